```python
import math
import jax, jax.numpy as jnp
from jax import lax
import numpy as np

D_MODEL = 1024
BATCH = 4
SEQ = 8192
DEPTH = 4

N_MIXERS = 2
N_SSD_LAYERS = (DEPTH + 1) // 2
N_ATTN_LAYERS = DEPTH // 2

SSD_EXPAND = 2
SSD_D_INNER = SSD_EXPAND * D_MODEL
SSD_HEADDIM = 64
SSD_HEADS = SSD_D_INNER // SSD_HEADDIM
SSD_D_STATE = 128
SSD_GROUPS = 8
SSD_HPG = SSD_HEADS // SSD_GROUPS
SSD_CONV = 5
SSD_CHUNK = 128
SSD_GN = SSD_GROUPS * SSD_D_STATE
SSD_CONV_DIM = SSD_D_INNER + 2 * SSD_GN
SSD_IN_DIM = SSD_D_INNER + SSD_CONV_DIM + 2 * SSD_HEADS

ATTN_HEADS = 8
ATTN_HEAD_DIM = 64
ATTN_V_HEAD = 2 * ATTN_HEAD_DIM
ATTN_QK_DIM = ATTN_HEADS * 2 * ATTN_HEAD_DIM
ATTN_V_DIM = ATTN_HEADS * ATTN_V_HEAD
ATTN_IN_DIM = 2 * ATTN_QK_DIM + 2 * ATTN_V_DIM
Q_BLOCK = 128

REL_BUCKETS = 32
REL_MAX_DIST = 128

RMS_EPS = 1e-6

kernel_name = "bidir_hybrid_ssd_diffattn_trunk"


def rmsnorm(x, w):
    x32 = x.astype(jnp.float32)
    y = x32 * lax.rsqrt(jnp.mean(x32 * x32, axis=-1, keepdims=True) + RMS_EPS)
    return (y * w.astype(jnp.float32)).astype(x.dtype)


def t5_bucket(rel):
    nb = REL_BUCKETS // 2
    max_exact = nb // 2
    ret = (rel > 0).astype(jnp.int32) * nb
    n = jnp.abs(rel)
    nf = jnp.maximum(n, 1).astype(jnp.float32)
    large = max_exact + (jnp.log(nf / max_exact) / math.log(REL_MAX_DIST / max_exact)
                         * (nb - max_exact)).astype(jnp.int32)
    large = jnp.minimum(large, nb - 1)
    return ret + jnp.where(n < max_exact, n, large)


def ssd_chunked(xin, a, bm, cm):
    b, l, g, r, p = xin.shape
    c = l // SSD_CHUNK
    q = SSD_CHUNK
    X = xin.reshape(b, c, q, g, r, p)
    B = bm.reshape(b, c, q, g, SSD_D_STATE)
    C = cm.reshape(b, c, q, g, SSD_D_STATE)
    A = jnp.transpose(a.reshape(b, c, q, g, r), (0, 3, 4, 1, 2))
    a_cs = jnp.cumsum(A, axis=-1)
    seg = a_cs[..., :, None] - a_cs[..., None, :]
    mask = jnp.tril(jnp.ones((q, q), dtype=bool))
    L = jnp.where(mask, jnp.exp(jnp.minimum(seg, 0.0)), 0.0)
    y_diag = jnp.einsum('bclgn,bcsgn,bgrcls,bcsgrp->bclgrp', C, B, L, X)
    decay_states = jnp.exp(a_cs[..., -1:] - a_cs)
    states = jnp.einsum('bcsgn,bgrcs,bcsgrp->bcgrpn', B, decay_states, X)
    chunk_decay = jnp.exp(a_cs[..., -1])

    def step(carry, inp):
        s_c, d_c = inp
        new = carry * d_c[..., None, None] + s_c
        return new, carry

    init = jnp.zeros_like(states[:, 0])
    _, prev = lax.scan(step, init, (jnp.moveaxis(states, 1, 0), jnp.moveaxis(chunk_decay, -1, 0)))
    prev = jnp.moveaxis(prev, 0, 1)
    y_off = jnp.einsum('bclgn,bcgrpn,bgrcl->bclgrp', C, prev, jnp.exp(a_cs))
    return (y_diag + y_off).reshape(b, l, g, r, p).astype(xin.dtype)


def ssd_mixer(h, w_in, conv_w, conv_b, dt_bias, a_log, d_skip, norm_w, w_out):
    b, l, _ = h.shape
    proj = h @ w_in
    z, xbc, dt = jnp.split(proj, [SSD_D_INNER, SSD_D_INNER + SSD_CONV_DIM], axis=-1)
    half = SSD_CONV // 2
    xbc = lax.conv_general_dilated(xbc, conv_w[:, None, :], (1,), [(half, half)],
                                   dimension_numbers=('NWC', 'WIO', 'NWC'),
                                   feature_group_count=SSD_CONV_DIM)
    xbc = jax.nn.silu(xbc + conv_b)
    xs, bm, cm = jnp.split(xbc, [SSD_D_INNER, SSD_D_INNER + SSD_GN], axis=-1)
    xs = xs.reshape(b, l, SSD_GROUPS, SSD_HPG, SSD_HEADDIM)
    bm = bm.reshape(b, l, SSD_GROUPS, SSD_D_STATE)
    cm = cm.reshape(b, l, SSD_GROUPS, SSD_D_STATE)
    dt = jax.nn.softplus((dt.reshape(b, l, 2, SSD_HEADS) + dt_bias).astype(jnp.float32))
    a = -jnp.exp(a_log.astype(jnp.float32))

    def direction(d_idx, flip):
        dtg = dt[:, :, d_idx].reshape(b, l, SSD_GROUPS, SSD_HPG)
        xin = xs * dtg[..., None].astype(xs.dtype)
        aa = dtg * a[d_idx].reshape(SSD_GROUPS, SSD_HPG)
        B_, C_ = bm, cm
        if flip:
            xin, aa, B_, C_ = (jnp.flip(t, axis=1) for t in (xin, aa, B_, C_))
        y = ssd_chunked(xin, aa, B_, C_)
        return jnp.flip(y, axis=1) if flip else y

    y = direction(0, False) + direction(1, True) \
        + xs * d_skip.reshape(SSD_GROUPS, SSD_HPG)[:, :, None]
    y = y.reshape(b, l, SSD_D_INNER)
    y = rmsnorm(y * jax.nn.silu(z), norm_w)
    return y @ w_out


def diff_attn_mixer(h, w_in, q_norm, k_norm, lam_vec, subln, w_out, rel_bias, layer_idx):
    b, l, _ = h.shape
    proj = h @ w_in
    q, k, v, z = jnp.split(proj, [ATTN_QK_DIM, 2 * ATTN_QK_DIM, 2 * ATTN_QK_DIM + ATTN_V_DIM], axis=-1)
    q = rmsnorm(q.reshape(b, l, ATTN_HEADS, 2, ATTN_HEAD_DIM), q_norm) * (ATTN_HEAD_DIM ** -0.5)
    k = rmsnorm(k.reshape(b, l, ATTN_HEADS, 2, ATTN_HEAD_DIM), k_norm)
    v = v.reshape(b, l, ATTN_HEADS, ATTN_V_HEAD)
    lam_init = 0.8 - 0.6 * math.exp(-0.3 * layer_idx)
    lv = lam_vec.astype(jnp.float32)
    lam = jnp.exp(jnp.sum(lv[0] * lv[1])) - jnp.exp(jnp.sum(lv[2] * lv[3])) + lam_init

    nb = l // Q_BLOCK
    qblocks = jnp.moveaxis(q.reshape(b, nb, Q_BLOCK, ATTN_HEADS, 2, ATTN_HEAD_DIM), 1, 0)
    kpos = jnp.arange(l, dtype=jnp.int32)

    def block(args):
        qb, i = args
        qpos = i * Q_BLOCK + jnp.arange(Q_BLOCK, dtype=jnp.int32)
        bias = rel_bias[t5_bucket(kpos[None, :] - qpos[:, None])]
        bias = jnp.transpose(bias, (2, 0, 1)).astype(jnp.float32)
        s = jnp.einsum('bqhtd,bkhtd->bhtqk', qb, k).astype(jnp.float32) + bias[None, :, None]
        pr = jax.nn.softmax(s, axis=-1)
        attn = pr[:, :, 0] - lam * pr[:, :, 1]
        return jnp.einsum('bhqk,bkhe->bqhe', attn.astype(v.dtype), v)

    o = lax.map(block, (qblocks, jnp.arange(nb, dtype=jnp.int32)))
    o = jnp.moveaxis(o, 0, 1).reshape(b, l, ATTN_HEADS, ATTN_V_HEAD)
    o = rmsnorm(o, subln) * (1.0 - lam_init)
    o = o.reshape(b, l, ATTN_V_DIM) * jax.nn.silu(z)
    return o @ w_out


def setup_inputs(seed: int = 0) -> dict:
    key = jax.random.key(seed)
    ks = jax.random.split(key, 20)
    f32 = jnp.float32
    nrm = lambda k, shape, scale: jax.random.normal(k, shape, f32) * scale
    NS, NA = N_SSD_LAYERS, N_ATTN_LAYERS
    x = jax.random.normal(ks[0], (BATCH, SEQ, D_MODEL), f32)
    norm_w = 1.0 + nrm(ks[1], (DEPTH, D_MODEL), 0.1)
    ssd_w_in = nrm(ks[2], (NS, D_MODEL, SSD_IN_DIM), D_MODEL ** -0.5)
    ssd_conv_w = nrm(ks[3], (NS, SSD_CONV, SSD_CONV_DIM), SSD_CONV ** -0.5)
    ssd_conv_b = nrm(ks[4], (NS, SSD_CONV_DIM), 0.02)
    dt0 = jnp.exp(jax.random.uniform(ks[5], (NS, 2, SSD_HEADS), f32)
                  * (math.log(0.1) - math.log(0.001)) + math.log(0.001))
    ssd_dt_bias = dt0 + jnp.log(-jnp.expm1(-dt0))
    ssd_a_log = jnp.log(jax.random.uniform(ks[6], (NS, 2, SSD_HEADS), f32, 1.0, 16.0))
    ssd_d = 1.0 + nrm(ks[7], (NS, SSD_HEADS), 0.1)
    ssd_norm_w = 1.0 + nrm(ks[8], (NS, SSD_D_INNER), 0.1)
    ssd_w_out = nrm(ks[9], (NS, SSD_D_INNER, D_MODEL), SSD_D_INNER ** -0.5)
    attn_w_in = nrm(ks[10], (NA, D_MODEL, ATTN_IN_DIM), D_MODEL ** -0.5)
    attn_q_norm = 1.0 + nrm(ks[11], (NA, ATTN_HEAD_DIM), 0.1)
    attn_k_norm = 1.0 + nrm(ks[12], (NA, ATTN_HEAD_DIM), 0.1)
    attn_lambda = nrm(ks[13], (NA, 4, ATTN_HEAD_DIM), 0.1)
    attn_subln = 1.0 + nrm(ks[14], (NA, ATTN_V_HEAD), 0.1)
    attn_w_out = nrm(ks[15], (NA, ATTN_V_DIM, D_MODEL), ATTN_V_DIM ** -0.5)
    rel_bias = nrm(ks[16], (REL_BUCKETS, ATTN_HEADS), 0.5)
    return {"x": x, "norm_w": norm_w,
            "ssd_w_in": ssd_w_in, "ssd_conv_w": ssd_conv_w, "ssd_conv_b": ssd_conv_b,
            "ssd_dt_bias": ssd_dt_bias, "ssd_a_log": ssd_a_log, "ssd_d": ssd_d,
            "ssd_norm_w": ssd_norm_w, "ssd_w_out": ssd_w_out,
            "attn_w_in": attn_w_in, "attn_q_norm": attn_q_norm, "attn_k_norm": attn_k_norm,
            "attn_lambda": attn_lambda, "attn_subln": attn_subln, "attn_w_out": attn_w_out,
            "rel_bias": rel_bias}


def reference(x, norm_w, ssd_w_in, ssd_conv_w, ssd_conv_b, ssd_dt_bias, ssd_a_log, ssd_d,
              ssd_norm_w, ssd_w_out, attn_w_in, attn_q_norm, attn_k_norm, attn_lambda,
              attn_subln, attn_w_out, rel_bias):
    h = x
    for i in range(DEPTH):
        hn = rmsnorm(h, norm_w[i])
        j = i // N_MIXERS
        if i % N_MIXERS == 0:
            out = ssd_mixer(hn, ssd_w_in[j], ssd_conv_w[j], ssd_conv_b[j], ssd_dt_bias[j],
                            ssd_a_log[j], ssd_d[j], ssd_norm_w[j], ssd_w_out[j])
        else:
            out = diff_attn_mixer(hn, attn_w_in[j], attn_q_norm[j], attn_k_norm[j],
                                  attn_lambda[j], attn_subln[j], attn_w_out[j], rel_bias, i)
        h = h + out.astype(h.dtype)
    return h
```

```python
import functools
import math

import jax
import jax.numpy as jnp
from jax import lax
from jax.experimental import pallas as pl
from jax.experimental.pallas import tpu as pltpu

F32 = jnp.float32
BF16 = jnp.bfloat16

RMS_EPS = 1e-6

D_MODEL = 1024
SSD_D_INNER = 2048
SSD_HEADDIM = 64
SSD_HEADS = 32
SSD_D_STATE = 128
SSD_GROUPS = 8
SSD_HPG = 4
SSD_CONV = 5
SSD_CHUNK = 128
SSD_GN = SSD_GROUPS * SSD_D_STATE
ATTN_HEADS = 8
ATTN_HEAD_DIM = 64
ATTN_V_HEAD = 128
ATTN_QK_DIM = 1024
ATTN_V_DIM = 1024
REL_BUCKETS = 32
REL_MAX_DIST = 128

V7X_VMEM_BYTES = 64 * 1024 * 1024
VMEM_LIMIT = V7X_VMEM_BYTES - 8 * 1024 * 1024
LANES = 128
BF16_SUBLANES = 16

ROW_TILE = 512
ATTN_TILE = 512
GS = SSD_HPG * SSD_HEADDIM


def _resident(shape):
    nd = len(shape)
    return pl.BlockSpec(shape, lambda *_: (0,) * nd, pipeline_mode=pl.Buffered(1))


def _silu(x):
    return x * (1.0 / (1.0 + jnp.exp(-x)))


def _softplus(x):
    return jnp.maximum(x, 0.0) + jnp.log(1.0 + jnp.exp(-jnp.abs(x)))


def _rms_rows(x, w_row):
    ms = jnp.mean(x * x, axis=-1, keepdims=True)
    return x * lax.rsqrt(ms + RMS_EPS) * w_row


def _dot(a, b):
    return jnp.dot(a, b, preferred_element_type=F32)


def _dot_nt(a, b):
    return lax.dot_general(a, b, (((1,), (1,)), ((), ())), preferred_element_type=F32)


def _ssd_in_kernel(h_ref, nw_ref, wz_ref, wx_ref, wb_ref, wc_ref, wdc_ref, wdr_ref,
                   z_ref, x_ref, b_ref, c_ref, dtc_ref, dtr_ref):
    xn = _rms_rows(h_ref[...], nw_ref[...]).astype(BF16)
    z_ref[...] = _dot(xn, wz_ref[...]).astype(BF16)
    x_ref[...] = _dot(xn, wx_ref[...]).astype(BF16)
    b_ref[...] = _dot(xn, wb_ref[...]).astype(BF16)
    c_ref[...] = _dot(xn, wc_ref[...]).astype(BF16)
    dtc_ref[...] = _dot(xn, wdc_ref[...])
    dtr_ref[...] = _dot_nt(wdr_ref[...], xn)


def _ssd_in_proj(h2, nw, wz, wx, wb, wc, wdc, wdr):
    t = h2.shape[0]
    tm = min(ROW_TILE, t)
    row = lambda n: pl.BlockSpec((tm, n), lambda i: (i, 0))
    return pl.pallas_call(
        _ssd_in_kernel,
        name="ssd_in_proj",
        grid=(t // tm,),
        in_specs=[row(D_MODEL), _resident(nw.shape), _resident(wz.shape), _resident(wx.shape),
                  _resident(wb.shape), _resident(wc.shape), _resident(wdc.shape),
                  _resident(wdr.shape)],
        out_specs=[row(SSD_D_INNER), row(SSD_D_INNER), row(SSD_GN), row(SSD_GN), row(LANES),
                   pl.BlockSpec((2 * SSD_HEADS, tm), lambda i: (0, i))],
        out_shape=[jax.ShapeDtypeStruct((t, SSD_D_INNER), BF16),
                   jax.ShapeDtypeStruct((t, SSD_D_INNER), BF16),
                   jax.ShapeDtypeStruct((t, SSD_GN), BF16),
                   jax.ShapeDtypeStruct((t, SSD_GN), BF16),
                   jax.ShapeDtypeStruct((t, LANES), F32),
                   jax.ShapeDtypeStruct((2 * SSD_HEADS, t), F32)],
        compiler_params=pltpu.CompilerParams(dimension_semantics=("arbitrary",),
                                             vmem_limit_bytes=VMEM_LIMIT),
    )(h2, nw, wz, wx, wb, wc, wdc, wdr)


def _split3(x):
    hi = x.astype(BF16)
    r1 = x - hi.astype(F32)
    mid = r1.astype(BF16)
    lo = (r1 - mid.astype(F32)).astype(BF16)
    return hi, mid, lo


def _ssd_scan_kernel(x_ref, b_ref, c_ref, dtc_ref, dtr_ref,
                     cwx_ref, cwb_ref, cwc_ref, cbx_ref, cbb_ref, cbc_ref,
                     biasc_ref, alogc_ref, biasr_ref, alogr_ref, d_ref,
                     y_ref, acc_ref, st_ref, *, seq):
    g = pl.program_id(1)
    q = SSD_CHUNK
    nchunk = seq // q
    halo = BF16_SUBLANES

    rows = lax.broadcasted_iota(jnp.int32, (q, q), 0)
    cols = lax.broadcasted_iota(jnp.int32, (q, q), 1)
    lane = lax.broadcasted_iota(jnp.int32, (q, LANES), 1)
    lane_gs = lax.broadcasted_iota(jnp.int32, (q, GS), 1)
    ek = lax.broadcasted_iota(jnp.int32, (2 * LANES, SSD_HPG * q), 0)
    en = lax.broadcasted_iota(jnp.int32, (2 * LANES, SSD_HPG * q), 1) >> 7

    a_col_scale = -jnp.exp(alogc_ref[...])
    a_row_scale = -jnp.exp(alogr_ref[...])
    d_row = d_ref[...]

    def conv_silu(ref, w_ref, bias_ref, c, s0):
        main = ref[0, pl.ds(s0, q), :].astype(F32)
        p0 = pl.multiple_of(jnp.maximum(s0 - halo, 0), halo)
        n0 = pl.multiple_of(jnp.minimum(s0 + q, seq - halo), halo)
        prev = ref[0, pl.ds(p0, halo), :].astype(F32) * (c > 0).astype(F32)
        nxt = ref[0, pl.ds(n0, halo), :].astype(F32) * (c < nchunk - 1).astype(F32)
        ext = jnp.concatenate([prev, main, nxt], axis=0)
        w = w_ref[...]
        out = bias_ref[...]
        half = SSD_CONV // 2
        for k in range(SSD_CONV):
            off = halo - half + k
            out = out + ext[off:off + q, :] * w[k:k + 1, :]
        return _silu(out)

    def chunk(c, rev):
        s0 = pl.multiple_of(c * q, q)
        d = 1 if rev else 0
        xs = conv_silu(x_ref, cwx_ref, cbx_ref, c, s0)
        bm = conv_silu(b_ref, cwb_ref, cbb_ref, c, s0)
        cm = conv_silu(c_ref, cwc_ref, cbc_ref, c, s0)
        xs_b = xs.astype(BF16)
        cm_b = cm.astype(BF16)
        gmat = _dot_nt(cm_b, bm.astype(BF16))
        bm_t = bm.T

        keep = (cols >= rows) if rev else (cols <= rows)
        keep_t = (rows >= cols) if rev else (rows <= cols)
        tri_b = keep.astype(BF16)

        dt_rows = _softplus(dtr_ref[pl.ds(pl.multiple_of(g * 8, 8), 8), pl.ds(s0, q)]
                            + biasr_ref[...])
        a_rows = dt_rows * a_row_scale
        cs_rows = jnp.dot(a_rows, keep_t.astype(F32), preferred_element_type=F32,
                          precision=lax.Precision.HIGHEST)
        a_cols = _softplus(dtc_ref[0, pl.ds(s0, q), :] + biasc_ref[...]) * a_col_scale
        hi, mid, lo = _split3(a_cols)
        r3 = _dot(tri_b, jnp.concatenate([hi, mid, lo], axis=1))
        cs_cols = r3[:, :LANES] + r3[:, LANES:2 * LANES] + r3[:, 2 * LANES:]

        hi, mid, lo = _split3(cs_cols)
        lhs = jnp.concatenate([jnp.where(lane < 64, hi, mid), lo], axis=1)
        col = d * SSD_HEADS + g * SSD_HPG + en
        emat = ((ek == col) | (ek == col + 64) | (ek == col + LANES)).astype(BF16)
        bc = _dot(lhs, emat)

        csx = jnp.concatenate(
            [jnp.where(lane < 64, bc[:, 0:q], bc[:, q:2 * q]),
             jnp.where(lane < 64, bc[:, 2 * q:3 * q], bc[:, 3 * q:4 * q])], axis=1)
        end_row = csx[0:1, :] if rev else csx[q - 1:q, :]

        state = st_ref[...]
        y = _dot(cm_b, state.astype(BF16)) * jnp.exp(csx)
        dstate = jnp.zeros((SSD_D_STATE, GS), F32)
        for r in range(SSD_HPG):
            row = d * SSD_HPG + r
            cs_r = cs_rows[row:row + 1, :]
            dt_r = dt_rows[row:row + 1, :]
            end_r = cs_r[:, 0:1] if rev else cs_r[:, q - 1:q]
            seg = bc[:, r * q:(r + 1) * q] - cs_r
            lmat = jnp.exp(jnp.minimum(seg, 0.0))
            mmat = jnp.where(keep, gmat * lmat * dt_r, 0.0).astype(BF16)
            xm = jnp.where((lane_gs >> 6) == r, xs_b, jnp.zeros_like(xs_b))
            y = y + _dot(mmat, xm)
            w_r = jnp.exp(end_r - cs_r) * dt_r
            dstate = dstate + _dot((bm_t * w_r).astype(BF16), xm)
        st_ref[...] = state * jnp.exp(end_row) + dstate
        if rev:
            y_ref[0, pl.ds(s0, q), :] = (acc_ref[pl.ds(s0, q), :] + y).astype(y_ref.dtype)
        else:
            acc_ref[pl.ds(s0, q), :] = y + xs * d_row

    st_ref[...] = jnp.zeros_like(st_ref)

    def fwd_body(c, carry):
        chunk(c, False)
        return carry

    lax.fori_loop(0, nchunk, fwd_body, 0)
    st_ref[...] = jnp.zeros_like(st_ref)

    def bwd_body(i, carry):
        chunk(nchunk - 1 - i, True)
        return carry

    lax.fori_loop(0, nchunk, bwd_body, 0)


def _ssd_scan(x3, b3, c3, dtc3, dtr, conv_w, conv_b, biasc, alogc, biasr, alogr, d_exp):
    bsz, seq, _ = x3.shape
    gsb = SSD_D_INNER // GS
    grp = lambda n, off: pl.BlockSpec((1, seq, n), lambda b, g: (b, 0, off + g))
    cw = lambda n, off: pl.BlockSpec((SSD_CONV, n), lambda b, g: (0, off + g))
    cb = lambda n, off: pl.BlockSpec((1, n), lambda b, g: (0, off + g))
    nb_x = SSD_D_INNER // SSD_D_STATE
    nb_c = nb_x + SSD_GN // SSD_D_STATE
    del gsb
    return pl.pallas_call(
        functools.partial(_ssd_scan_kernel, seq=seq),
        name="ssd_scan",
        grid=(bsz, SSD_GROUPS),
        in_specs=[grp(GS, 0), grp(SSD_D_STATE, 0), grp(SSD_D_STATE, 0),
                  pl.BlockSpec((1, seq, LANES), lambda b, g: (b, 0, 0)),
                  pl.BlockSpec((2 * SSD_HEADS, seq), lambda b, g: (0, b)),
                  cw(GS, 0), cw(SSD_D_STATE, nb_x), cw(SSD_D_STATE, nb_c),
                  cb(GS, 0), cb(SSD_D_STATE, nb_x), cb(SSD_D_STATE, nb_c),
                  pl.BlockSpec((1, LANES), lambda b, g: (0, 0)),
                  pl.BlockSpec((1, LANES), lambda b, g: (0, 0)),
                  pl.BlockSpec((8, LANES), lambda b, g: (g, 0)),
                  pl.BlockSpec((8, LANES), lambda b, g: (g, 0)),
                  pl.BlockSpec((1, GS), lambda b, g: (0, g))],
        out_specs=pl.BlockSpec((1, seq, GS), lambda b, g: (b, 0, g)),
        out_shape=jax.ShapeDtypeStruct((bsz, seq, SSD_D_INNER), BF16),
        scratch_shapes=[pltpu.VMEM((seq, GS), F32), pltpu.VMEM((SSD_D_STATE, GS), F32)],
        compiler_params=pltpu.CompilerParams(dimension_semantics=("arbitrary", "arbitrary"),
                                             vmem_limit_bytes=VMEM_LIMIT),
    )(x3, b3, c3, dtc3, dtr, conv_w, conv_w, conv_w, conv_b, conv_b, conv_b,
      biasc, alogc, biasr, alogr, d_exp)


def _ssd_out_kernel(y_ref, z_ref, nw_ref, w_ref, h_ref, o_ref):
    u = y_ref[...].astype(F32) * _silu(z_ref[...].astype(F32))
    un = _rms_rows(u, nw_ref[...]).astype(BF16)
    o_ref[...] = h_ref[...] + _dot(un, w_ref[...])


def _ssd_out_proj(y2, z2, nw, w, h2):
    t = h2.shape[0]
    tm = min(ROW_TILE, t)
    row = lambda n: pl.BlockSpec((tm, n), lambda i: (i, 0))
    return pl.pallas_call(
        _ssd_out_kernel,
        name="ssd_out_proj",
        grid=(t // tm,),
        in_specs=[row(SSD_D_INNER), row(SSD_D_INNER), _resident(nw.shape), _resident(w.shape),
                  row(D_MODEL)],
        out_specs=row(D_MODEL),
        out_shape=jax.ShapeDtypeStruct((t, D_MODEL), F32),
        compiler_params=pltpu.CompilerParams(dimension_semantics=("arbitrary",),
                                             vmem_limit_bytes=VMEM_LIMIT),
    )(y2, z2, nw, w, h2)


def _attn_in_kernel(h_ref, nw_ref, wq_ref, wk_ref, wv_ref, wz_ref, qn_ref, kn_ref, ones_ref,
                    q_ref, k_ref, v_ref, z_ref):
    xn = _rms_rows(h_ref[...], nw_ref[...]).astype(BF16)

    def head_norm(w_ref, gain_ref):
        y = _dot(xn, w_ref[...])
        ss = _dot((y * y).astype(BF16), ones_ref[...])
        return (y * lax.rsqrt(ss * (1.0 / ATTN_HEAD_DIM) + RMS_EPS) * gain_ref[...]).astype(BF16)

    q_ref[...] = head_norm(wq_ref, qn_ref)
    k_ref[...] = head_norm(wk_ref, kn_ref)
    v_ref[...] = _dot(xn, wv_ref[...]).astype(BF16)
    z_ref[...] = _dot(xn, wz_ref[...]).astype(BF16)


def _attn_in_proj(h2, nw, wq, wk, wv, wz, qn, kn, ones_bd):
    t = h2.shape[0]
    tm = min(ROW_TILE, t)
    row = lambda n: pl.BlockSpec((tm, n), lambda i: (i, 0))
    return pl.pallas_call(
        _attn_in_kernel,
        name="attn_in_proj",
        grid=(t // tm,),
        in_specs=[row(D_MODEL)] + [_resident(a.shape) for a in (nw, wq, wk, wv, wz, qn, kn, ones_bd)],
        out_specs=[row(ATTN_QK_DIM), row(ATTN_QK_DIM), row(ATTN_V_DIM), row(ATTN_V_DIM)],
        out_shape=[jax.ShapeDtypeStruct((t, ATTN_QK_DIM), BF16)] * 2
                  + [jax.ShapeDtypeStruct((t, ATTN_V_DIM), BF16)] * 2,
        compiler_params=pltpu.CompilerParams(dimension_semantics=("arbitrary",),
                                             vmem_limit_bytes=VMEM_LIMIT),
    )(h2, nw, wq, wk, wv, wz, qn, kn, ones_bd)


def _attn_kernel(thr_ref, rb_ref, q_ref, k_ref, v_ref, z_ref, lam_ref, subln_ref, o_ref,
                 bias_ref, *, seq, tile, lam_init):
    h = pl.program_id(0)
    b = pl.program_id(1)
    qi = pl.program_id(2)
    nk = seq // tile
    half_buckets = REL_BUCKETS // 2

    @pl.when((b == 0) & (qi == 0))
    def _build_bias():
        rel0 = (lax.broadcasted_iota(jnp.int32, (tile, tile), 1)
                - lax.broadcasted_iota(jnp.int32, (tile, tile), 0))
        for idx in range(5):
            rel = rel0 + (idx - 2) * tile
            n = jnp.abs(rel)
            neg = jnp.full((tile, tile), rb_ref[h], F32)
            pos = jnp.full((tile, tile), rb_ref[half_buckets * ATTN_HEADS + h], F32)
            for t in range(1, half_buckets):
                ge = n >= thr_ref[t]
                neg = jnp.where(ge, rb_ref[t * ATTN_HEADS + h], neg)
                pos = jnp.where(ge, rb_ref[(half_buckets + t) * ATTN_HEADS + h], pos)
            bias_ref[idx] = jnp.where(rel > 0, pos, neg)

    lv = lam_ref[...]
    lam = (jnp.exp(jnp.sum(lv[0:1] * lv[1:2], axis=-1, keepdims=True))
           - jnp.exp(jnp.sum(lv[2:3] * lv[3:4], axis=-1, keepdims=True)) + lam_init)

    qt = q_ref[0]
    lane = lax.broadcasted_iota(jnp.int32, (tile, ATTN_V_HEAD), 1)
    zero = jnp.zeros_like(qt)
    qs = (jnp.where(lane < ATTN_HEAD_DIM, qt, zero), jnp.where(lane >= ATTN_HEAD_DIM, qt, zero))

    def body(ki, carry):
        k0 = pl.multiple_of(ki * tile, tile)
        kt = k_ref[0, pl.ds(k0, tile), :]
        vt = v_ref[0, pl.ds(k0, tile), :]
        bias = bias_ref[jnp.clip(ki - qi, -2, 2) + 2]
        new = []
        for t in range(2):
            m_prev, l_prev, acc_prev = carry[t]
            s = _dot_nt(qs[t], kt) + bias
            m_new = jnp.maximum(m_prev, jnp.max(s, axis=-1, keepdims=True))
            alpha = jnp.exp(m_prev - m_new)
            p = jnp.exp(s - m_new)
            l_new = alpha * l_prev + jnp.sum(p, axis=-1, keepdims=True)
            acc_new = alpha * acc_prev + _dot(p.astype(BF16), vt)
            new.append((m_new, l_new, acc_new))
        return tuple(new)

    init = tuple((jnp.full((tile, 1), -jnp.inf, F32), jnp.zeros((tile, 1), F32),
                  jnp.zeros((tile, ATTN_V_HEAD), F32)) for _ in range(2))
    (_, l1, a1), (_, l2, a2) = lax.fori_loop(0, nk, body, init)
    o = a1 / l1 - lam * (a2 / l2)
    o = _rms_rows(o, subln_ref[...]) * (1.0 - lam_init)
    o_ref[0] = (o * _silu(z_ref[0].astype(F32))).astype(o_ref.dtype)


def _attention(thr, rb_flat, q3, k3, v3, z3, lam_vec, subln, lam_init):
    bsz, seq, _ = q3.shape
    tile = min(ATTN_TILE, seq)
    nq = seq // tile
    smem = pl.BlockSpec(memory_space=pltpu.SMEM)
    qspec = pl.BlockSpec((1, tile, ATTN_V_HEAD), lambda h, b, i: (b, i, h))
    kvspec = pl.BlockSpec((1, seq, ATTN_V_HEAD), lambda h, b, i: (b, 0, h))
    return pl.pallas_call(
        functools.partial(_attn_kernel, seq=seq, tile=tile, lam_init=lam_init),
        name="diff_attention",
        grid=(ATTN_HEADS, bsz, nq),
        in_specs=[smem, smem, qspec, kvspec, kvspec, qspec,
                  pl.BlockSpec(lam_vec.shape, lambda h, b, i: (0, 0)),
                  pl.BlockSpec(subln.shape, lambda h, b, i: (0, 0))],
        out_specs=qspec,
        out_shape=jax.ShapeDtypeStruct((bsz, seq, ATTN_V_DIM), BF16),
        scratch_shapes=[pltpu.VMEM((5, tile, tile), F32)],
        compiler_params=pltpu.CompilerParams(
            dimension_semantics=("arbitrary", "arbitrary", "arbitrary"),
            vmem_limit_bytes=VMEM_LIMIT),
    )(thr, rb_flat, q3, k3, v3, z3, lam_vec, subln)


def _attn_out_kernel(o_ref, w_ref, h_ref, out_ref):
    out_ref[...] = h_ref[...] + _dot(o_ref[...], w_ref[...])


def _attn_out_proj(o2, w, h2):
    t = h2.shape[0]
    tm = min(ROW_TILE, t)
    row = lambda n: pl.BlockSpec((tm, n), lambda i: (i, 0))
    return pl.pallas_call(
        _attn_out_kernel,
        name="attn_out_proj",
        grid=(t // tm,),
        in_specs=[row(ATTN_V_DIM), _resident(w.shape), row(D_MODEL)],
        out_specs=row(D_MODEL),
        out_shape=jax.ShapeDtypeStruct((t, D_MODEL), F32),
        compiler_params=pltpu.CompilerParams(dimension_semantics=("arbitrary",),
                                             vmem_limit_bytes=VMEM_LIMIT),
    )(o2, w, h2)


def _ssd_layer(h2, bsz, seq, nw, w_in, conv_w, conv_b, dt_bias, a_log, d_skip, norm_w, w_out):
    wb16 = w_in.astype(BF16)
    o_x = SSD_D_INNER
    o_b = 2 * SSD_D_INNER
    o_c = o_b + SSD_GN
    o_dt = o_c + SSD_GN
    wdt = wb16[:, o_dt:]
    wdc = jnp.concatenate([wdt, wdt], axis=1)
    perm = lambda a: a.reshape(2, SSD_GROUPS, SSD_HPG).transpose(1, 0, 2).reshape(2 * SSD_HEADS)
    wdr = wdt.T.reshape(2, SSD_GROUPS, SSD_HPG, D_MODEL).transpose(1, 0, 2, 3).reshape(
        2 * SSD_HEADS, D_MODEL)
    z2, x2, b2, c2, dtc, dtr = _ssd_in_proj(
        h2, nw.reshape(1, -1), wb16[:, :o_x], wb16[:, o_x:o_b], wb16[:, o_b:o_c],
        wb16[:, o_c:o_dt], wdc, wdr)

    flat = lambda a: a.reshape(2 * SSD_HEADS).astype(F32)
    biasc = jnp.tile(flat(dt_bias), 2).reshape(1, LANES)
    alogc = jnp.tile(flat(a_log), 2).reshape(1, LANES)
    biasr = jnp.broadcast_to(perm(flat(dt_bias))[:, None], (2 * SSD_HEADS, LANES))
    alogr = jnp.broadcast_to(perm(flat(a_log))[:, None], (2 * SSD_HEADS, LANES))
    d_exp = jnp.repeat(d_skip.astype(F32), SSD_HEADDIM).reshape(1, SSD_D_INNER)

    y3 = _ssd_scan(x2.reshape(bsz, seq, -1), b2.reshape(bsz, seq, -1), c2.reshape(bsz, seq, -1),
                   dtc.reshape(bsz, seq, LANES), dtr, conv_w_full(conv_w), conv_b.reshape(1, -1),
                   biasc, alogc, biasr, alogr, d_exp)
    return _ssd_out_proj(y3.reshape(bsz * seq, -1), z2, norm_w.reshape(1, -1),
                         w_out.astype(BF16), h2)


def conv_w_full(conv_w):
    return conv_w.astype(F32)


def _t5_thresholds():
    nb = REL_BUCKETS // 2
    max_exact = nb // 2
    n = jnp.arange(REL_MAX_DIST + 1, dtype=jnp.int32)
    nf = jnp.maximum(n, 1).astype(F32)
    large = max_exact + (jnp.log(nf / max_exact) / math.log(REL_MAX_DIST / max_exact)
                         * (nb - max_exact)).astype(jnp.int32)
    bucket = jnp.where(n < max_exact, n, jnp.minimum(large, nb - 1))
    t = jnp.arange(nb, dtype=jnp.int32)
    return jnp.sum((bucket[None, :] < t[:, None]).astype(jnp.int32), axis=1)


def _attn_layer(h2, bsz, seq, nw, w_in, q_norm, k_norm, lam_vec, subln, w_out, thr, rb_flat,
                ones_bd, layer_idx):
    wb16 = w_in.astype(BF16)
    reps = ATTN_QK_DIM // ATTN_HEAD_DIM
    qn = (jnp.tile(q_norm.astype(F32), reps) * (ATTN_HEAD_DIM ** -0.5)).reshape(1, -1)
    kn = jnp.tile(k_norm.astype(F32), reps).reshape(1, -1)
    q2, k2, v2, z2 = _attn_in_proj(
        h2, nw.reshape(1, -1), wb16[:, :ATTN_QK_DIM], wb16[:, ATTN_QK_DIM:2 * ATTN_QK_DIM],
        wb16[:, 2 * ATTN_QK_DIM:2 * ATTN_QK_DIM + ATTN_V_DIM],
        wb16[:, 2 * ATTN_QK_DIM + ATTN_V_DIM:], qn, kn, ones_bd)
    lam_init = 0.8 - 0.6 * math.exp(-0.3 * layer_idx)
    r3 = lambda a: a.reshape(bsz, seq, -1)
    o3 = _attention(thr, rb_flat, r3(q2), r3(k2), r3(v2), r3(z2), lam_vec.astype(F32),
                    subln.reshape(1, -1).astype(F32), lam_init)
    return _attn_out_proj(o3.reshape(bsz * seq, -1), w_out.astype(BF16), h2)


def kernel(x, norm_w, ssd_w_in, ssd_conv_w, ssd_conv_b, ssd_dt_bias, ssd_a_log, ssd_d, ssd_norm_w, ssd_w_out, attn_w_in, attn_q_norm, attn_k_norm, attn_lambda, attn_subln, attn_w_out, rel_bias):
    bsz, seq, _ = x.shape
    depth = norm_w.shape[0]
    h2 = x.reshape(bsz * seq, D_MODEL)
    thr = _t5_thresholds()
    rb_flat = rel_bias.astype(F32).reshape(-1)
    blk = jnp.arange(ATTN_QK_DIM, dtype=jnp.int32) // ATTN_HEAD_DIM
    ones_bd = (blk[:, None] == blk[None, :]).astype(BF16)
    for i in range(depth):
        j = i // 2
        if i % 2 == 0:
            h2 = _ssd_layer(h2, bsz, seq, norm_w[i], ssd_w_in[j], ssd_conv_w[j], ssd_conv_b[j],
                            ssd_dt_bias[j], ssd_a_log[j], ssd_d[j], ssd_norm_w[j], ssd_w_out[j])
        else:
            h2 = _attn_layer(h2, bsz, seq, norm_w[i], attn_w_in[j], attn_q_norm[j], attn_k_norm[j],
                             attn_lambda[j], attn_subln[j], attn_w_out[j], thr, rb_flat, ones_bd, i)
    return h2.reshape(bsz, seq, D_MODEL)
```

```python
import functools
import math

import jax
import jax.numpy as jnp
from jax import lax
from jax.experimental import pallas as pl
from jax.experimental.pallas import tpu as pltpu

F32 = jnp.float32
BF16 = jnp.bfloat16

RMS_EPS = 1e-6

D_MODEL = 1024
SSD_D_INNER = 2048
SSD_HEADDIM = 64
SSD_HEADS = 32
SSD_D_STATE = 128
SSD_GROUPS = 8
SSD_HPG = 4
SSD_CONV = 5
SSD_CHUNK = 128
SSD_GN = SSD_GROUPS * SSD_D_STATE
ATTN_HEADS = 8
ATTN_HEAD_DIM = 64
ATTN_V_HEAD = 128
ATTN_QK_DIM = 1024
ATTN_V_DIM = 1024
REL_BUCKETS = 32
REL_MAX_DIST = 128

V7X_VMEM_BYTES = 64 * 1024 * 1024
VMEM_LIMIT = V7X_VMEM_BYTES - 8 * 1024 * 1024
LANES = 128
BF16_SUBLANES = 16

ROW_TILE = 512
HALO = BF16_SUBLANES
CONV_COLS = 512
ATTN_TILE = 512
ATTN_ROWS_A = 128
ATTN_ROWS_B = 64
LOG2E = 1.4426950408889634
GS = SSD_HPG * SSD_HEADDIM


def _resident(shape):
    nd = len(shape)
    return pl.BlockSpec(shape, lambda *_: (0,) * nd, pipeline_mode=pl.Buffered(1))


def _silu(x):
    return x * (1.0 / (1.0 + jnp.exp(-x)))


def _softplus(x):
    return jnp.maximum(x, 0.0) + jnp.log(1.0 + jnp.exp(-jnp.abs(x)))


def _rms_rows(x, w_row):
    ms = jnp.mean(x * x, axis=-1, keepdims=True)
    return x * lax.rsqrt(ms + RMS_EPS) * w_row


def _dot(a, b):
    return jnp.dot(a, b, preferred_element_type=F32)


def _dot_nt(a, b):
    return lax.dot_general(a, b, (((1,), (1,)), ((), ())), preferred_element_type=F32)


def _ssd_in_kernel(h_ref, hp_ref, hn_ref, nw_ref, wz_ref, wxbc_ref, wdc_ref, wdr_ref,
                   cw_ref, cb_ref, z_ref, x_ref, b_ref, c_ref, dtc_ref, dtr_ref, pre_ref, *, seq):
    i = pl.program_id(0)
    tm = h_ref.shape[0]
    nw = nw_ref[...]
    at_start = (i * tm) % seq == 0
    at_end = ((i + 1) * tm) % seq == 0
    xn = _rms_rows(h_ref[...], nw).astype(BF16)
    xp = (_rms_rows(hp_ref[...], nw) * jnp.where(at_start, 0.0, 1.0)).astype(BF16)
    xq = (_rms_rows(hn_ref[...], nw) * jnp.where(at_end, 0.0, 1.0)).astype(BF16)
    ext = jnp.concatenate([xp, xn, xq], axis=0)

    z_ref[...] = _dot(xn, wz_ref[...]).astype(BF16)
    dtc_ref[...] = _dot(xn, wdc_ref[...])
    dtr_ref[...] = _dot_nt(wdr_ref[...], xn)

    half = SSD_CONV // 2
    outs = ((x_ref, 0, SSD_D_INNER), (b_ref, SSD_D_INNER, SSD_GN),
            (c_ref, SSD_D_INNER + SSD_GN, SSD_GN))
    for o_ref, base, width in outs:
        for j in range(width // CONV_COLS):
            lo = base + j * CONV_COLS
            pre_ref[...] = _dot(ext, wxbc_ref[:, lo:lo + CONV_COLS])
            w = cw_ref[:, lo:lo + CONV_COLS]
            acc = cb_ref[:, lo:lo + CONV_COLS]
            for k in range(SSD_CONV):
                off = HALO - half + k
                acc = acc + pre_ref[off:off + tm, :] * w[k:k + 1, :]
            o_ref[:, j * CONV_COLS:(j + 1) * CONV_COLS] = _silu(acc).astype(BF16)


def _ssd_in_proj(h2, seq, nw, wz, wxbc, wdc, wdr, conv_w, conv_b):
    t = h2.shape[0]
    tm = min(ROW_TILE, seq)
    assert seq % tm == 0 and tm % HALO == 0
    per = tm // HALO
    last = t // HALO - 1
    row = lambda n: pl.BlockSpec((tm, n), lambda i: (i, 0))
    return pl.pallas_call(
        functools.partial(_ssd_in_kernel, seq=seq),
        name="ssd_in_proj",
        grid=(t // tm,),
        in_specs=[row(D_MODEL),
                  pl.BlockSpec((HALO, D_MODEL), lambda i: (jnp.maximum(i * per - 1, 0), 0)),
                  pl.BlockSpec((HALO, D_MODEL), lambda i: (jnp.minimum((i + 1) * per, last), 0)),
                  _resident(nw.shape), _resident(wz.shape), _resident(wxbc.shape),
                  _resident(wdc.shape), _resident(wdr.shape), _resident(conv_w.shape),
                  _resident(conv_b.shape)],
        out_specs=[row(SSD_D_INNER), row(SSD_D_INNER), row(SSD_GN), row(SSD_GN), row(LANES),
                   pl.BlockSpec((2 * SSD_HEADS, tm), lambda i: (0, i))],
        out_shape=[jax.ShapeDtypeStruct((t, SSD_D_INNER), BF16),
                   jax.ShapeDtypeStruct((t, SSD_D_INNER), BF16),
                   jax.ShapeDtypeStruct((t, SSD_GN), BF16),
                   jax.ShapeDtypeStruct((t, SSD_GN), BF16),
                   jax.ShapeDtypeStruct((t, LANES), F32),
                   jax.ShapeDtypeStruct((2 * SSD_HEADS, t), F32)],
        scratch_shapes=[pltpu.VMEM((tm + 2 * HALO, CONV_COLS), F32)],
        compiler_params=pltpu.CompilerParams(dimension_semantics=("arbitrary",),
                                             vmem_limit_bytes=VMEM_LIMIT),
    )(h2, h2, h2, nw, wz, wxbc, wdc, wdr, conv_w, conv_b)


def _split3(x):
    hi = x.astype(BF16)
    r1 = x - hi.astype(F32)
    mid = r1.astype(BF16)
    lo = (r1 - mid.astype(F32)).astype(BF16)
    return hi, mid, lo


def _ssd_scan_kernel(x_ref, b_ref, c_ref, dtc_ref, dtr_ref,
                     biasc_ref, alogc_ref, biasr_ref, alogr_ref, d_ref,
                     y_ref, acc_ref, stf_ref, stb_ref, e_ref, *, seq):
    g = pl.program_id(1)
    q = SSD_CHUNK
    nchunk = seq // q

    rows = lax.broadcasted_iota(jnp.int32, (q, q), 0)
    cols = lax.broadcasted_iota(jnp.int32, (q, q), 1)
    lane = lax.broadcasted_iota(jnp.int32, (q, LANES), 1)
    lane_gs = lax.broadcasted_iota(jnp.int32, (q, GS), 1)

    ek = lax.broadcasted_iota(jnp.int32, (2 * LANES, SSD_HPG * q), 0)
    en = lax.broadcasted_iota(jnp.int32, (2 * LANES, SSD_HPG * q), 1) >> 7
    for d in range(2):
        col = d * SSD_HEADS + g * SSD_HPG + en
        e_ref[d] = ((ek == col) | (ek == col + 64) | (ek == col + LANES)).astype(BF16)

    a_col_scale = -jnp.exp(alogc_ref[...])
    a_row_scale = -jnp.exp(alogr_ref[...])
    d_row = d_ref[...]

    def chunk(c, rev):
        s0 = pl.multiple_of(c * q, q)
        d = 1 if rev else 0
        st_ref = stb_ref if rev else stf_ref
        xs_b = x_ref[0, pl.ds(s0, q), :]
        bm_b = b_ref[0, pl.ds(s0, q), :]
        cm_b = c_ref[0, pl.ds(s0, q), :]
        gmat = _dot_nt(cm_b, bm_b)
        bm_t = bm_b.astype(F32).T

        keep = (cols >= rows) if rev else (cols <= rows)
        keep_t = (rows >= cols) if rev else (rows <= cols)
        tri_b = keep.astype(BF16)

        dt_rows = _softplus(dtr_ref[pl.ds(pl.multiple_of(g * 8, 8), 8), pl.ds(s0, q)]
                            + biasr_ref[...])
        a_rows = dt_rows * a_row_scale
        cs_rows = jnp.dot(a_rows, keep_t.astype(F32), preferred_element_type=F32,
                          precision=lax.Precision.HIGHEST)
        a_cols = _softplus(dtc_ref[0, pl.ds(s0, q), :] + biasc_ref[...]) * a_col_scale
        hi, mid, lo = _split3(a_cols)
        r3 = _dot(tri_b, jnp.concatenate([hi, mid, lo], axis=1))
        cs_cols = r3[:, :LANES] + r3[:, LANES:2 * LANES] + r3[:, 2 * LANES:]

        hi, mid, lo = _split3(cs_cols)
        lhs = jnp.concatenate([jnp.where(lane < 64, hi, mid), lo], axis=1)
        bc = _dot(lhs, e_ref[d])

        csx = jnp.concatenate(
            [jnp.where(lane < 64, bc[:, 0:q], bc[:, q:2 * q]),
             jnp.where(lane < 64, bc[:, 2 * q:3 * q], bc[:, 3 * q:4 * q])], axis=1)
        end_row = csx[0:1, :] if rev else csx[q - 1:q, :]

        state = st_ref[...]
        y = _dot(cm_b, state.astype(BF16)) * jnp.exp(csx)
        dstate = jnp.zeros((SSD_D_STATE, GS), F32)
        for r in range(SSD_HPG):
            row = d * SSD_HPG + r
            cs_r = cs_rows[row:row + 1, :]
            dt_r = dt_rows[row:row + 1, :]
            end_r = cs_r[:, 0:1] if rev else cs_r[:, q - 1:q]
            seg = bc[:, r * q:(r + 1) * q] - cs_r
            lmat = jnp.exp(jnp.minimum(seg, 0.0))
            mmat = jnp.where(keep, gmat * lmat * dt_r, 0.0).astype(BF16)
            xm = jnp.where((lane_gs >> 6) == r, xs_b, jnp.zeros_like(xs_b))
            y = y + _dot(mmat, xm)
            w_r = jnp.exp(end_r - cs_r) * dt_r
            dstate = dstate + _dot((bm_t * w_r).astype(BF16), xm)
        st_ref[...] = state * jnp.exp(end_row) + dstate
        return y, xs_b

    def step(i, finish):
        cf = i
        cb = nchunk - 1 - i
        yf, xs_f = chunk(cf, False)
        yb, _ = chunk(cb, True)
        yf = yf + xs_f.astype(F32) * d_row
        rf = pl.ds(pl.multiple_of(cf * q, q), q)
        rb = pl.ds(pl.multiple_of(cb * q, q), q)
        if finish:
            y_ref[0, rf, :] = (acc_ref[rf, :] + yf).astype(y_ref.dtype)
            y_ref[0, rb, :] = (acc_ref[rb, :] + yb).astype(y_ref.dtype)
        else:
            acc_ref[rf, :] = yf
            acc_ref[rb, :] = yb

    stf_ref[...] = jnp.zeros_like(stf_ref)
    stb_ref[...] = jnp.zeros_like(stb_ref)

    def first_half(i, carry):
        step(i, False)
        return carry

    def second_half(i, carry):
        step(i, True)
        return carry

    lax.fori_loop(0, nchunk // 2, first_half, 0)
    lax.fori_loop(nchunk // 2, nchunk, second_half, 0)


def _ssd_scan(x3, b3, c3, dtc3, dtr, biasc, alogc, biasr, alogr, d_exp):
    bsz, seq, _ = x3.shape
    assert seq % (2 * SSD_CHUNK) == 0
    grp = lambda n: pl.BlockSpec((1, seq, n), lambda b, g: (b, 0, g))
    return pl.pallas_call(
        functools.partial(_ssd_scan_kernel, seq=seq),
        name="ssd_scan",
        grid=(bsz, SSD_GROUPS),
        in_specs=[grp(GS), grp(SSD_D_STATE), grp(SSD_D_STATE),
                  pl.BlockSpec((1, seq, LANES), lambda b, g: (b, 0, 0)),
                  pl.BlockSpec((2 * SSD_HEADS, seq), lambda b, g: (0, b)),
                  pl.BlockSpec((1, LANES), lambda b, g: (0, 0)),
                  pl.BlockSpec((1, LANES), lambda b, g: (0, 0)),
                  pl.BlockSpec((8, LANES), lambda b, g: (g, 0)),
                  pl.BlockSpec((8, LANES), lambda b, g: (g, 0)),
                  pl.BlockSpec((1, GS), lambda b, g: (0, g))],
        out_specs=pl.BlockSpec((1, seq, GS), lambda b, g: (b, 0, g)),
        out_shape=jax.ShapeDtypeStruct((bsz, seq, SSD_D_INNER), BF16),
        scratch_shapes=[pltpu.VMEM((seq, GS), F32),
                        pltpu.VMEM((SSD_D_STATE, GS), F32),
                        pltpu.VMEM((SSD_D_STATE, GS), F32),
                        pltpu.VMEM((2, 2 * LANES, SSD_HPG * SSD_CHUNK), BF16)],
        compiler_params=pltpu.CompilerParams(dimension_semantics=("arbitrary", "arbitrary"),
                                             vmem_limit_bytes=VMEM_LIMIT),
    )(x3, b3, c3, dtc3, dtr, biasc, alogc, biasr, alogr, d_exp)


def _ssd_out_kernel(y_ref, z_ref, nw_ref, w_ref, h_ref, o_ref):
    u = y_ref[...].astype(F32) * _silu(z_ref[...].astype(F32))
    un = _rms_rows(u, nw_ref[...]).astype(BF16)
    o_ref[...] = h_ref[...] + _dot(un, w_ref[...])


def _ssd_out_proj(y2, z2, nw, w, h2):
    t = h2.shape[0]
    tm = min(ROW_TILE, t)
    row = lambda n: pl.BlockSpec((tm, n), lambda i: (i, 0))
    return pl.pallas_call(
        _ssd_out_kernel,
        name="ssd_out_proj",
        grid=(t // tm,),
        in_specs=[row(SSD_D_INNER), row(SSD_D_INNER), _resident(nw.shape), _resident(w.shape),
                  row(D_MODEL)],
        out_specs=row(D_MODEL),
        out_shape=jax.ShapeDtypeStruct((t, D_MODEL), F32),
        compiler_params=pltpu.CompilerParams(dimension_semantics=("arbitrary",),
                                             vmem_limit_bytes=VMEM_LIMIT),
    )(y2, z2, nw, w, h2)


def _attn_in_kernel(h_ref, nw_ref, wqt_ref, wk_ref, wvt_ref, wz_ref, qn_ref, kn_ref, ones_ref,
                    qt_ref, k_ref, vt_ref, z_ref):
    xn = _rms_rows(h_ref[...], nw_ref[...]).astype(BF16)
    inv_d = 1.0 / ATTN_HEAD_DIM

    yq = _dot_nt(wqt_ref[...], xn)
    ssq = _dot(ones_ref[...], (yq * yq).astype(BF16))
    qt_ref[...] = (yq * lax.rsqrt(ssq * inv_d + RMS_EPS) * qn_ref[...]).astype(BF16)

    yk = _dot(xn, wk_ref[...])
    ssk = _dot((yk * yk).astype(BF16), ones_ref[...])
    k_ref[...] = (yk * lax.rsqrt(ssk * inv_d + RMS_EPS) * kn_ref[...]).astype(BF16)

    vt_ref[...] = _dot_nt(wvt_ref[...], xn).astype(BF16)
    z_ref[...] = _dot(xn, wz_ref[...]).astype(BF16)


def _attn_in_proj(h2, nw, wqt, wk, wvt, wz, qn_cols, kn, ones_bd):
    t = h2.shape[0]
    tm = qn_cols.shape[1]
    row = lambda n: pl.BlockSpec((tm, n), lambda i: (i, 0))
    col = lambda n: pl.BlockSpec((n, tm), lambda i: (0, i))
    return pl.pallas_call(
        _attn_in_kernel,
        name="attn_in_proj",
        grid=(t // tm,),
        in_specs=[row(D_MODEL)] + [_resident(a.shape)
                                   for a in (nw, wqt, wk, wvt, wz, qn_cols, kn, ones_bd)],
        out_specs=[col(ATTN_QK_DIM), row(ATTN_QK_DIM), col(ATTN_V_DIM), row(ATTN_V_DIM)],
        out_shape=[jax.ShapeDtypeStruct((ATTN_QK_DIM, t), BF16),
                   jax.ShapeDtypeStruct((t, ATTN_QK_DIM), BF16),
                   jax.ShapeDtypeStruct((ATTN_V_DIM, t), BF16),
                   jax.ShapeDtypeStruct((t, ATTN_V_DIM), BF16)],
        compiler_params=pltpu.CompilerParams(dimension_semantics=("arbitrary",),
                                             vmem_limit_bytes=VMEM_LIMIT),
    )(h2, nw, wqt, wk, wvt, wz, qn_cols, kn, ones_bd)


def _attn_kernel(thr_ref, rb_ref, qt_ref, k_ref, vt_ref, z_ref, lam_ref, subln_ref, o_ref,
                 bias_ref, sa_ref, sb_ref, p_ref, m_ref, l_ref, acc_ref, *, seq, tile, lam_init):
    h = pl.program_id(0)
    b = pl.program_id(1)
    qi = pl.program_id(2)
    nk = seq // tile
    half_buckets = REL_BUCKETS // 2
    sub = 8

    @pl.when((b == 0) & (qi == 0))
    def _build_bias():
        rel0 = (lax.broadcasted_iota(jnp.int32, (tile, tile), 0)
                - lax.broadcasted_iota(jnp.int32, (tile, tile), 1))
        for idx in range(5):
            rel = rel0 + (idx - 2) * tile
            n = jnp.abs(rel)
            neg = jnp.full((tile, tile), rb_ref[h], F32)
            pos = jnp.full((tile, tile), rb_ref[half_buckets * ATTN_HEADS + h], F32)
            for t in range(1, half_buckets):
                ge = n >= thr_ref[t]
                neg = jnp.where(ge, rb_ref[t * ATTN_HEADS + h], neg)
                pos = jnp.where(ge, rb_ref[(half_buckets + t) * ATTN_HEADS + h], pos)
            bias_ref[idx] = jnp.where(rel > 0, pos, neg) * LOG2E

    q_t = qt_ref[...]
    row = lax.broadcasted_iota(jnp.int32, (ATTN_V_HEAD, tile), 0)
    zero = jnp.zeros_like(q_t)
    qs = (jnp.where(row < ATTN_HEAD_DIM, q_t, zero), jnp.where(row >= ATTN_HEAD_DIM, q_t, zero))

    m_ref[...] = jnp.full(m_ref.shape, -jnp.inf, F32)
    l_ref[...] = jnp.zeros(l_ref.shape, F32)
    acc_ref[...] = jnp.zeros(acc_ref.shape, F32)

    def score_block(kc, t, r, s_out, m8):
        rows = pl.ds(r * ATTN_ROWS_A, ATTN_ROWS_A)
        kt = k_ref[0, pl.ds(pl.multiple_of(kc * tile, tile) + r * ATTN_ROWS_A, ATTN_ROWS_A), :]
        bidx = jnp.clip(kc - qi, -2, 2) + 2
        sb = _dot(kt, qs[t]) + bias_ref[bidx, rows, :]
        s_out[t, rows, :] = sb
        for i in range(ATTN_ROWS_A // sub):
            m8 = jnp.maximum(m8, sb[i * sub:(i + 1) * sub, :])
        return m8

    def step(kc_next, s_next, kc, s_cur, m8s):
        vt = vt_ref[:, pl.ds(pl.multiple_of(kc * tile, tile), tile)]
        nblk = tile // ATTN_ROWS_A
        per = ATTN_ROWS_A // ATTN_ROWS_B
        m8s_next = []
        for t in range(2):
            m_old = m_ref[t]
            m_new = jnp.maximum(m_old, jnp.max(m8s[t], axis=0, keepdims=True))
            alpha = jnp.exp2(m_old - m_new)
            l8 = jnp.zeros((sub, tile), F32)
            m8 = jnp.full((sub, tile), -jnp.inf, F32)
            for r in range(nblk):
                m8 = score_block(kc_next, t, r, s_next, m8)
                for rb in range(r * per, (r + 1) * per):
                    rows = pl.ds(rb * ATTN_ROWS_B, ATTN_ROWS_B)
                    p = jnp.exp2(s_cur[t, rows, :] - m_new)
                    for i in range(ATTN_ROWS_B // sub):
                        l8 = l8 + p[i * sub:(i + 1) * sub, :]
                    p_ref[t, rows, :] = p.astype(BF16)
            l_ref[t] = alpha * l_ref[t] + l8
            acc_ref[t] = alpha * acc_ref[t] + _dot(vt, p_ref[t])
            m_ref[t] = m_new
            m8s_next.append(m8)
        return tuple(m8s_next)

    def chunk_pair(j, m8s):
        c0 = 2 * j
        m8s_odd = step(c0 + 1, sb_ref, c0, sa_ref, m8s)
        return step(jnp.minimum(c0 + 2, nk - 1), sa_ref, c0 + 1, sb_ref, m8s_odd)

    m8s0 = []
    for t in range(2):
        m8 = jnp.full((sub, tile), -jnp.inf, F32)
        for r in range(tile // ATTN_ROWS_A):
            m8 = score_block(0, t, r, sa_ref, m8)
        m8s0.append(m8)
    lax.fori_loop(0, nk // 2, chunk_pair, tuple(m8s0))

    lv = lam_ref[...]
    lam = (jnp.exp(jnp.sum(lv[0:1] * lv[1:2], axis=-1, keepdims=True))
           - jnp.exp(jnp.sum(lv[2:3] * lv[3:4], axis=-1, keepdims=True)) + lam_init)
    o1 = acc_ref[0] / jnp.sum(l_ref[0], axis=0, keepdims=True)
    o2 = acc_ref[1] / jnp.sum(l_ref[1], axis=0, keepdims=True)
    o = (o1 - lam * o2).T
    o = _rms_rows(o, subln_ref[...]) * (1.0 - lam_init)
    o_ref[0] = (o * _silu(z_ref[0].astype(F32))).astype(o_ref.dtype)


def _attention(thr, rb_flat, qt, k3, vt, z3, lam_vec, subln, lam_init):
    bsz, seq, _ = k3.shape
    tile = min(ATTN_TILE, seq)
    nq = seq // tile
    assert tile > REL_MAX_DIST and seq % (2 * tile) == 0
    smem = pl.BlockSpec(memory_space=pltpu.SMEM)
    qspec = pl.BlockSpec((ATTN_V_HEAD, tile), lambda h, b, i: (h, b * nq + i))
    kspec = pl.BlockSpec((1, seq, ATTN_V_HEAD), lambda h, b, i: (b, 0, h))
    vspec = pl.BlockSpec((ATTN_V_HEAD, seq), lambda h, b, i: (h, b))
    zspec = pl.BlockSpec((1, tile, ATTN_V_HEAD), lambda h, b, i: (b, i, h))
    return pl.pallas_call(
        functools.partial(_attn_kernel, seq=seq, tile=tile, lam_init=lam_init),
        name="diff_attention",
        grid=(ATTN_HEADS, bsz, nq),
        in_specs=[smem, smem, qspec, kspec, vspec, zspec,
                  pl.BlockSpec(lam_vec.shape, lambda h, b, i: (0, 0)),
                  pl.BlockSpec(subln.shape, lambda h, b, i: (0, 0))],
        out_specs=zspec,
        out_shape=jax.ShapeDtypeStruct((bsz, seq, ATTN_V_DIM), BF16),
        scratch_shapes=[pltpu.VMEM((5, tile, tile), F32),
                        pltpu.VMEM((2, tile, tile), F32),
                        pltpu.VMEM((2, tile, tile), F32),
                        pltpu.VMEM((2, tile, tile), BF16),
                        pltpu.VMEM((2, 1, tile), F32),
                        pltpu.VMEM((2, 8, tile), F32),
                        pltpu.VMEM((2, ATTN_V_HEAD, tile), F32)],
        compiler_params=pltpu.CompilerParams(
            dimension_semantics=("arbitrary", "arbitrary", "arbitrary"),
            vmem_limit_bytes=VMEM_LIMIT),
    )(thr, rb_flat, qt, k3, vt, z3, lam_vec, subln)


def _attn_out_kernel(o_ref, w_ref, h_ref, out_ref):
    out_ref[...] = h_ref[...] + _dot(o_ref[...], w_ref[...])


def _attn_out_proj(o2, w, h2):
    t = h2.shape[0]
    tm = min(ROW_TILE, t)
    row = lambda n: pl.BlockSpec((tm, n), lambda i: (i, 0))
    return pl.pallas_call(
        _attn_out_kernel,
        name="attn_out_proj",
        grid=(t // tm,),
        in_specs=[row(ATTN_V_DIM), _resident(w.shape), row(D_MODEL)],
        out_specs=row(D_MODEL),
        out_shape=jax.ShapeDtypeStruct((t, D_MODEL), F32),
        compiler_params=pltpu.CompilerParams(dimension_semantics=("arbitrary",),
                                             vmem_limit_bytes=VMEM_LIMIT),
    )(o2, w, h2)


def _ssd_layer(h2, bsz, seq, nw, w_in, conv_w, conv_b, dt_bias, a_log, d_skip, norm_w, w_out):
    wb16 = w_in.astype(BF16)
    o_x = SSD_D_INNER
    o_b = 2 * SSD_D_INNER
    o_c = o_b + SSD_GN
    o_dt = o_c + SSD_GN
    wdt = wb16[:, o_dt:]
    wdc = jnp.concatenate([wdt, wdt], axis=1)
    perm = lambda a: a.reshape(2, SSD_GROUPS, SSD_HPG).transpose(1, 0, 2).reshape(2 * SSD_HEADS)
    wdr = wdt.T.reshape(2, SSD_GROUPS, SSD_HPG, D_MODEL).transpose(1, 0, 2, 3).reshape(
        2 * SSD_HEADS, D_MODEL)
    del o_b, o_c
    z2, x2, b2, c2, dtc, dtr = _ssd_in_proj(
        h2, seq, nw.reshape(1, -1), wb16[:, :o_x], wb16[:, o_x:o_dt], wdc, wdr,
        conv_w.astype(F32), conv_b.reshape(1, -1).astype(F32))

    flat = lambda a: a.reshape(2 * SSD_HEADS).astype(F32)
    biasc = jnp.tile(flat(dt_bias), 2).reshape(1, LANES)
    alogc = jnp.tile(flat(a_log), 2).reshape(1, LANES)
    biasr = jnp.broadcast_to(perm(flat(dt_bias))[:, None], (2 * SSD_HEADS, LANES))
    alogr = jnp.broadcast_to(perm(flat(a_log))[:, None], (2 * SSD_HEADS, LANES))
    d_exp = jnp.repeat(d_skip.astype(F32), SSD_HEADDIM).reshape(1, SSD_D_INNER)

    y3 = _ssd_scan(x2.reshape(bsz, seq, -1), b2.reshape(bsz, seq, -1), c2.reshape(bsz, seq, -1),
                   dtc.reshape(bsz, seq, LANES), dtr, biasc, alogc, biasr, alogr, d_exp)
    return _ssd_out_proj(y3.reshape(bsz * seq, -1), z2, norm_w.reshape(1, -1),
                         w_out.astype(BF16), h2)


def _t5_thresholds():
    nb = REL_BUCKETS // 2
    max_exact = nb // 2
    n = jnp.arange(REL_MAX_DIST + 1, dtype=jnp.int32)
    nf = jnp.maximum(n, 1).astype(F32)
    large = max_exact + (jnp.log(nf / max_exact) / math.log(REL_MAX_DIST / max_exact)
                         * (nb - max_exact)).astype(jnp.int32)
    bucket = jnp.where(n < max_exact, n, jnp.minimum(large, nb - 1))
    t = jnp.arange(nb, dtype=jnp.int32)
    return jnp.sum((bucket[None, :] < t[:, None]).astype(jnp.int32), axis=1)


def _attn_layer(h2, bsz, seq, nw, w_in, q_norm, k_norm, lam_vec, subln, w_out, thr, rb_flat,
                ones_bd, layer_idx):
    wb16 = w_in.astype(BF16)
    reps = ATTN_QK_DIM // ATTN_HEAD_DIM
    tm = min(ROW_TILE, bsz * seq)
    qn = jnp.tile(q_norm.astype(F32), reps) * (ATTN_HEAD_DIM ** -0.5 * LOG2E)
    qn_cols = jnp.broadcast_to(qn[:, None], (ATTN_QK_DIM, tm))
    kn = jnp.tile(k_norm.astype(F32), reps).reshape(1, -1)
    qt, k2, vt, z2 = _attn_in_proj(
        h2, nw.reshape(1, -1), wb16[:, :ATTN_QK_DIM].T, wb16[:, ATTN_QK_DIM:2 * ATTN_QK_DIM],
        wb16[:, 2 * ATTN_QK_DIM:2 * ATTN_QK_DIM + ATTN_V_DIM].T,
        wb16[:, 2 * ATTN_QK_DIM + ATTN_V_DIM:], qn_cols, kn, ones_bd)
    lam_init = 0.8 - 0.6 * math.exp(-0.3 * layer_idx)
    r3 = lambda a: a.reshape(bsz, seq, -1)
    o3 = _attention(thr, rb_flat, qt, r3(k2), vt, r3(z2), lam_vec.astype(F32),
                    subln.reshape(1, -1).astype(F32), lam_init)
    return _attn_out_proj(o3.reshape(bsz * seq, -1), w_out.astype(BF16), h2)


def kernel(x, norm_w, ssd_w_in, ssd_conv_w, ssd_conv_b, ssd_dt_bias, ssd_a_log, ssd_d, ssd_norm_w, ssd_w_out, attn_w_in, attn_q_norm, attn_k_norm, attn_lambda, attn_subln, attn_w_out, rel_bias):
    bsz, seq, _ = x.shape
    depth = norm_w.shape[0]
    h2 = x.reshape(bsz * seq, D_MODEL)
    thr = _t5_thresholds()
    rb_flat = rel_bias.astype(F32).reshape(-1)
    blk = jnp.arange(ATTN_QK_DIM, dtype=jnp.int32) // ATTN_HEAD_DIM
    ones_bd = (blk[:, None] == blk[None, :]).astype(BF16)
    for i in range(depth):
        j = i // 2
        if i % 2 == 0:
            h2 = _ssd_layer(h2, bsz, seq, norm_w[i], ssd_w_in[j], ssd_conv_w[j], ssd_conv_b[j],
                            ssd_dt_bias[j], ssd_a_log[j], ssd_d[j], ssd_norm_w[j], ssd_w_out[j])
        else:
            h2 = _attn_layer(h2, bsz, seq, norm_w[i], attn_w_in[j], attn_q_norm[j], attn_k_norm[j],
                             attn_lambda[j], attn_subln[j], attn_w_out[j], thr, rb_flat, ones_bd, i)
    return h2.reshape(bsz, seq, D_MODEL)
```

```python
import functools
import math

import jax
import jax.numpy as jnp
from jax import lax
from jax.experimental import pallas as pl
from jax.experimental.pallas import tpu as pltpu

F32 = jnp.float32
BF16 = jnp.bfloat16

RMS_EPS = 1e-6

D_MODEL = 1024
SSD_D_INNER = 2048
SSD_HEADDIM = 64
SSD_HEADS = 32
SSD_D_STATE = 128
SSD_GROUPS = 8
SSD_HPG = 4
SSD_CONV = 5
SSD_CHUNK = 128
SSD_GN = SSD_GROUPS * SSD_D_STATE
ATTN_HEADS = 8
ATTN_HEAD_DIM = 64
ATTN_V_HEAD = 128
ATTN_QK_DIM = 1024
ATTN_V_DIM = 1024
REL_BUCKETS = 32
REL_MAX_DIST = 128

V7X_VMEM_BYTES = 64 * 1024 * 1024
VMEM_LIMIT = V7X_VMEM_BYTES - 8 * 1024 * 1024
LANES = 128
BF16_SUBLANES = 16

ROW_TILE = 512
HALO = BF16_SUBLANES
CONV_COLS = 512
ATTN_TILE = 512
ATTN_ROWS_A = 128
ATTN_ROWS_B = 64
ATTN_ONES_ROWS = BF16_SUBLANES
LOG2E = 1.4426950408889634
GS = SSD_HPG * SSD_HEADDIM


def _resident(shape):
    nd = len(shape)
    return pl.BlockSpec(shape, lambda *_: (0,) * nd, pipeline_mode=pl.Buffered(1))


def _silu(x):
    return x * (1.0 / (1.0 + jnp.exp(-x)))


def _softplus(x):
    return jnp.maximum(x, 0.0) + jnp.log(1.0 + jnp.exp(-jnp.abs(x)))


def _rms_rows(x, w_row):
    ms = jnp.mean(x * x, axis=-1, keepdims=True)
    return x * lax.rsqrt(ms + RMS_EPS) * w_row


def _dot(a, b):
    return jnp.dot(a, b, preferred_element_type=F32)


def _dot_nt(a, b):
    return lax.dot_general(a, b, (((1,), (1,)), ((), ())), preferred_element_type=F32)


def _ssd_in_kernel(h_ref, hp_ref, hn_ref, nw_ref, wz_ref, wxbc_ref, wdc_ref, wdr_ref,
                   cw_ref, cb_ref, z_ref, x_ref, b_ref, c_ref, dtc_ref, dtr_ref, pre_ref, *, seq):
    i = pl.program_id(0)
    tm = h_ref.shape[0]
    nw = nw_ref[...]
    at_start = (i * tm) % seq == 0
    at_end = ((i + 1) * tm) % seq == 0
    xn = _rms_rows(h_ref[...], nw).astype(BF16)
    xp = (_rms_rows(hp_ref[...], nw) * jnp.where(at_start, 0.0, 1.0)).astype(BF16)
    xq = (_rms_rows(hn_ref[...], nw) * jnp.where(at_end, 0.0, 1.0)).astype(BF16)
    ext = jnp.concatenate([xp, xn, xq], axis=0)

    z_ref[...] = _dot(xn, wz_ref[...]).astype(BF16)
    dtc_ref[...] = _dot(xn, wdc_ref[...])
    dtr_ref[...] = _dot_nt(wdr_ref[...], xn)

    half = SSD_CONV // 2
    outs = ((x_ref, 0, SSD_D_INNER), (b_ref, SSD_D_INNER, SSD_GN),
            (c_ref, SSD_D_INNER + SSD_GN, SSD_GN))
    for o_ref, base, width in outs:
        for j in range(width // CONV_COLS):
            lo = base + j * CONV_COLS
            pre_ref[...] = _dot(ext, wxbc_ref[:, lo:lo + CONV_COLS])
            w = cw_ref[:, lo:lo + CONV_COLS]
            acc = cb_ref[:, lo:lo + CONV_COLS]
            for k in range(SSD_CONV):
                off = HALO - half + k
                acc = acc + pre_ref[off:off + tm, :] * w[k:k + 1, :]
            o_ref[:, j * CONV_COLS:(j + 1) * CONV_COLS] = _silu(acc).astype(BF16)


def _ssd_in_proj(h2, seq, nw, wz, wxbc, wdc, wdr, conv_w, conv_b):
    t = h2.shape[0]
    tm = min(ROW_TILE, seq)
    assert seq % tm == 0 and tm % HALO == 0
    per = tm // HALO
    last = t // HALO - 1
    row = lambda n: pl.BlockSpec((tm, n), lambda i: (i, 0))
    return pl.pallas_call(
        functools.partial(_ssd_in_kernel, seq=seq),
        name="ssd_in_proj",
        grid=(t // tm,),
        in_specs=[row(D_MODEL),
                  pl.BlockSpec((HALO, D_MODEL), lambda i: (jnp.maximum(i * per - 1, 0), 0)),
                  pl.BlockSpec((HALO, D_MODEL), lambda i: (jnp.minimum((i + 1) * per, last), 0)),
                  _resident(nw.shape), _resident(wz.shape), _resident(wxbc.shape),
                  _resident(wdc.shape), _resident(wdr.shape), _resident(conv_w.shape),
                  _resident(conv_b.shape)],
        out_specs=[row(SSD_D_INNER), row(SSD_D_INNER), row(SSD_GN), row(SSD_GN), row(LANES),
                   pl.BlockSpec((2 * SSD_HEADS, tm), lambda i: (0, i))],
        out_shape=[jax.ShapeDtypeStruct((t, SSD_D_INNER), BF16),
                   jax.ShapeDtypeStruct((t, SSD_D_INNER), BF16),
                   jax.ShapeDtypeStruct((t, SSD_GN), BF16),
                   jax.ShapeDtypeStruct((t, SSD_GN), BF16),
                   jax.ShapeDtypeStruct((t, LANES), F32),
                   jax.ShapeDtypeStruct((2 * SSD_HEADS, t), F32)],
        scratch_shapes=[pltpu.VMEM((tm + 2 * HALO, CONV_COLS), F32)],
        compiler_params=pltpu.CompilerParams(dimension_semantics=("arbitrary",),
                                             vmem_limit_bytes=VMEM_LIMIT),
    )(h2, h2, h2, nw, wz, wxbc, wdc, wdr, conv_w, conv_b)


def _split3(x):
    hi = x.astype(BF16)
    r1 = x - hi.astype(F32)
    mid = r1.astype(BF16)
    lo = (r1 - mid.astype(F32)).astype(BF16)
    return hi, mid, lo


def _ssd_scan_kernel(x_ref, b_ref, c_ref, dtc_ref, dtr_ref,
                     biasc_ref, alogc_ref, biasr_ref, alogr_ref, d_ref,
                     y_ref, acc_ref, stf_ref, stb_ref, e_ref, *, seq):
    g = pl.program_id(1)
    q = SSD_CHUNK
    nchunk = seq // q

    rows = lax.broadcasted_iota(jnp.int32, (q, q), 0)
    cols = lax.broadcasted_iota(jnp.int32, (q, q), 1)
    lane = lax.broadcasted_iota(jnp.int32, (q, LANES), 1)
    lane_gs = lax.broadcasted_iota(jnp.int32, (q, GS), 1)

    ek = lax.broadcasted_iota(jnp.int32, (2 * LANES, SSD_HPG * q), 0)
    en = lax.broadcasted_iota(jnp.int32, (2 * LANES, SSD_HPG * q), 1) >> 7
    for d in range(2):
        col = d * SSD_HEADS + g * SSD_HPG + en
        e_ref[d] = ((ek == col) | (ek == col + 64) | (ek == col + LANES)).astype(BF16)

    a_col_scale = -jnp.exp(alogc_ref[...])
    a_row_scale = -jnp.exp(alogr_ref[...])
    d_row = d_ref[...]

    def front(c, rev):
        s0 = pl.multiple_of(c * q, q)
        xs_b = x_ref[0, pl.ds(s0, q), :]
        bm_b = b_ref[0, pl.ds(s0, q), :]
        cm_b = c_ref[0, pl.ds(s0, q), :]
        gmat = _dot_nt(cm_b, bm_b)
        bm_t = bm_b.astype(F32).T

        keep = (cols >= rows) if rev else (cols <= rows)
        keep_t = (rows >= cols) if rev else (rows <= cols)
        dt_rows = _softplus(dtr_ref[pl.ds(pl.multiple_of(g * 8, 8), 8), pl.ds(s0, q)]
                            + biasr_ref[...])
        cs_rows = jnp.dot(dt_rows * a_row_scale, keep_t.astype(F32), preferred_element_type=F32,
                          precision=lax.Precision.HIGHEST)
        a_cols = _softplus(dtc_ref[0, pl.ds(s0, q), :] + biasc_ref[...]) * a_col_scale
        hi, mid, lo = _split3(a_cols)
        r3 = _dot(keep.astype(BF16), jnp.concatenate([hi, mid, lo], axis=1))
        cs_cols = r3[:, :LANES] + r3[:, LANES:2 * LANES] + r3[:, 2 * LANES:]
        return dict(xs_b=xs_b, cm_b=cm_b, gmat=gmat, bm_t=bm_t, keep=keep, dt_rows=dt_rows,
                    cs_rows=cs_rows, cs_cols=cs_cols)

    def middle(f, rev):
        d = 1 if rev else 0
        hi, mid, lo = _split3(f["cs_cols"])
        lhs = jnp.concatenate([jnp.where(lane < 64, hi, mid), lo], axis=1)
        bc = _dot(lhs, e_ref[d])
        csx = jnp.concatenate(
            [jnp.where(lane < 64, bc[:, 0:q], bc[:, q:2 * q]),
             jnp.where(lane < 64, bc[:, 2 * q:3 * q], bc[:, 3 * q:4 * q])], axis=1)
        mmats, bts, xms = [], [], []
        for r in range(SSD_HPG):
            row = d * SSD_HPG + r
            cs_r = f["cs_rows"][row:row + 1, :]
            dt_r = f["dt_rows"][row:row + 1, :]
            end_r = cs_r[:, 0:1] if rev else cs_r[:, q - 1:q]
            seg = bc[:, r * q:(r + 1) * q] - cs_r
            lmat = jnp.exp(jnp.minimum(seg, 0.0))
            mmats.append(jnp.where(f["keep"], f["gmat"] * lmat * dt_r, 0.0).astype(BF16))
            xms.append(jnp.where((lane_gs >> 6) == r, f["xs_b"], jnp.zeros_like(f["xs_b"])))
            w_r = jnp.exp(end_r - cs_r) * dt_r
            bts.append((f["bm_t"] * w_r).astype(BF16))
        pairs = [(jnp.concatenate(mmats[r:r + 2], axis=1), jnp.concatenate(bts[r:r + 2], axis=1),
                  jnp.concatenate(xms[r:r + 2], axis=0)) for r in range(0, SSD_HPG, 2)]
        return dict(cm_b=f["cm_b"], xs_b=f["xs_b"], decay=jnp.exp(csx),
                    end_decay=jnp.exp(csx[0:1, :] if rev else csx[q - 1:q, :]), pairs=pairs)

    def back(m, rev):
        st_ref = stb_ref if rev else stf_ref
        state = st_ref[...]
        y = _dot(m["cm_b"], state.astype(BF16)) * m["decay"]
        dstate = jnp.zeros((SSD_D_STATE, GS), F32)
        for mm, bt, xpair in m["pairs"]:
            y = y + _dot(mm, xpair)
            dstate = dstate + _dot(bt, xpair)
        st_ref[...] = state * m["end_decay"] + dstate
        return y

    def step(i, finish):
        work = ((2 * i, False), (nchunk - 1 - 2 * i, True),
                (2 * i + 1, False), (nchunk - 2 - 2 * i, True))
        fronts = [front(c, rev) for c, rev in work]
        mids = [middle(f, rev) for f, (_, rev) in zip(fronts, work)]
        for m, (c, rev) in zip(mids, work):
            y = back(m, rev)
            if not rev:
                y = y + m["xs_b"].astype(F32) * d_row
            rs = pl.ds(pl.multiple_of(c * q, q), q)
            if finish:
                y_ref[0, rs, :] = (acc_ref[rs, :] + y).astype(y_ref.dtype)
            else:
                acc_ref[rs, :] = y

    stf_ref[...] = jnp.zeros_like(stf_ref)
    stb_ref[...] = jnp.zeros_like(stb_ref)

    def first_half(i, carry):
        step(i, False)
        return carry

    def second_half(i, carry):
        step(i, True)
        return carry

    lax.fori_loop(0, nchunk // 4, first_half, 0)
    lax.fori_loop(nchunk // 4, nchunk // 2, second_half, 0)


def _ssd_scan(x3, b3, c3, dtc3, dtr, biasc, alogc, biasr, alogr, d_exp):
    bsz, seq, _ = x3.shape
    assert seq % (4 * SSD_CHUNK) == 0
    grp = lambda n: pl.BlockSpec((1, seq, n), lambda b, g: (b, 0, g))
    return pl.pallas_call(
        functools.partial(_ssd_scan_kernel, seq=seq),
        name="ssd_scan",
        grid=(bsz, SSD_GROUPS),
        in_specs=[grp(GS), grp(SSD_D_STATE), grp(SSD_D_STATE),
                  pl.BlockSpec((1, seq, LANES), lambda b, g: (b, 0, 0)),
                  pl.BlockSpec((2 * SSD_HEADS, seq), lambda b, g: (0, b)),
                  pl.BlockSpec((1, LANES), lambda b, g: (0, 0)),
                  pl.BlockSpec((1, LANES), lambda b, g: (0, 0)),
                  pl.BlockSpec((8, LANES), lambda b, g: (g, 0)),
                  pl.BlockSpec((8, LANES), lambda b, g: (g, 0)),
                  pl.BlockSpec((1, GS), lambda b, g: (0, g))],
        out_specs=pl.BlockSpec((1, seq, GS), lambda b, g: (b, 0, g)),
        out_shape=jax.ShapeDtypeStruct((bsz, seq, SSD_D_INNER), BF16),
        scratch_shapes=[pltpu.VMEM((seq, GS), F32),
                        pltpu.VMEM((SSD_D_STATE, GS), F32),
                        pltpu.VMEM((SSD_D_STATE, GS), F32),
                        pltpu.VMEM((2, 2 * LANES, SSD_HPG * SSD_CHUNK), BF16)],
        compiler_params=pltpu.CompilerParams(dimension_semantics=("arbitrary", "arbitrary"),
                                             vmem_limit_bytes=VMEM_LIMIT),
    )(x3, b3, c3, dtc3, dtr, biasc, alogc, biasr, alogr, d_exp)


def _ssd_out_kernel(y_ref, z_ref, nw_ref, w_ref, h_ref, o_ref):
    u = y_ref[...].astype(F32) * _silu(z_ref[...].astype(F32))
    un = _rms_rows(u, nw_ref[...]).astype(BF16)
    o_ref[...] = h_ref[...] + _dot(un, w_ref[...])


def _ssd_out_proj(y2, z2, nw, w, h2):
    t = h2.shape[0]
    tm = min(ROW_TILE, t)
    row = lambda n: pl.BlockSpec((tm, n), lambda i: (i, 0))
    return pl.pallas_call(
        _ssd_out_kernel,
        name="ssd_out_proj",
        grid=(t // tm,),
        in_specs=[row(SSD_D_INNER), row(SSD_D_INNER), _resident(nw.shape), _resident(w.shape),
                  row(D_MODEL)],
        out_specs=row(D_MODEL),
        out_shape=jax.ShapeDtypeStruct((t, D_MODEL), F32),
        compiler_params=pltpu.CompilerParams(dimension_semantics=("arbitrary",),
                                             vmem_limit_bytes=VMEM_LIMIT),
    )(y2, z2, nw, w, h2)


def _attn_in_kernel(h_ref, nw_ref, wqt_ref, wk_ref, wvt_ref, wz_ref, qn_ref, kn_ref, ones_ref,
                    qt_ref, k_ref, vt_ref, z_ref):
    xn = _rms_rows(h_ref[...], nw_ref[...]).astype(BF16)
    inv_d = 1.0 / ATTN_HEAD_DIM

    yq = _dot_nt(wqt_ref[...], xn)
    ssq = _dot(ones_ref[...], (yq * yq).astype(BF16))
    qt_ref[...] = (yq * lax.rsqrt(ssq * inv_d + RMS_EPS) * qn_ref[...]).astype(BF16)

    yk = _dot(xn, wk_ref[...])
    ssk = _dot((yk * yk).astype(BF16), ones_ref[...])
    k_ref[...] = (yk * lax.rsqrt(ssk * inv_d + RMS_EPS) * kn_ref[...]).astype(BF16)

    vt_ref[...] = _dot_nt(wvt_ref[...], xn).astype(BF16)
    z_ref[...] = _dot(xn, wz_ref[...]).astype(BF16)


def _attn_in_proj(h2, nw, wqt, wk, wvt, wz, qn_cols, kn, ones_bd):
    t = h2.shape[0]
    tm = qn_cols.shape[1]
    row = lambda n: pl.BlockSpec((tm, n), lambda i: (i, 0))
    col = lambda n: pl.BlockSpec((n, tm), lambda i: (0, i))
    return pl.pallas_call(
        _attn_in_kernel,
        name="attn_in_proj",
        grid=(t // tm,),
        in_specs=[row(D_MODEL)] + [_resident(a.shape)
                                   for a in (nw, wqt, wk, wvt, wz, qn_cols, kn, ones_bd)],
        out_specs=[col(ATTN_QK_DIM), row(ATTN_QK_DIM), col(ATTN_V_DIM), row(ATTN_V_DIM)],
        out_shape=[jax.ShapeDtypeStruct((ATTN_QK_DIM, t), BF16),
                   jax.ShapeDtypeStruct((t, ATTN_QK_DIM), BF16),
                   jax.ShapeDtypeStruct((ATTN_V_DIM, t), BF16),
                   jax.ShapeDtypeStruct((t, ATTN_V_DIM), BF16)],
        compiler_params=pltpu.CompilerParams(dimension_semantics=("arbitrary",),
                                             vmem_limit_bytes=VMEM_LIMIT),
    )(h2, nw, wqt, wk, wvt, wz, qn_cols, kn, ones_bd)


def _attn_kernel(thr_ref, rb_ref, qt_ref, k_ref, vt_ref, z_ref, lam_ref, subln_ref, o_ref,
                 bias_ref, sa_ref, sb_ref, p_ref, m_ref, acc_ref, vte_ref, *, seq, tile, lam_init):
    h = pl.program_id(0)
    b = pl.program_id(1)
    qi = pl.program_id(2)
    nk = seq // tile
    half_buckets = REL_BUCKETS // 2
    sub = 8

    @pl.when((b == 0) & (qi == 0))
    def _build_bias():
        rel0 = (lax.broadcasted_iota(jnp.int32, (tile, tile), 0)
                - lax.broadcasted_iota(jnp.int32, (tile, tile), 1))
        for idx in range(3):
            rel = rel0 + (idx - 1) * tile
            n = jnp.abs(rel)
            neg = jnp.full((tile, tile), rb_ref[h], F32)
            pos = jnp.full((tile, tile), rb_ref[half_buckets * ATTN_HEADS + h], F32)
            for t in range(1, half_buckets):
                ge = n >= thr_ref[t]
                neg = jnp.where(ge, rb_ref[t * ATTN_HEADS + h], neg)
                pos = jnp.where(ge, rb_ref[(half_buckets + t) * ATTN_HEADS + h], pos)
            bias_ref[idx] = jnp.where(rel > 0, pos, neg) * LOG2E

    @pl.when(qi == 0)
    def _extend_v():
        vte_ref[0:ATTN_V_HEAD, :] = vt_ref[...]
        vte_ref[ATTN_V_HEAD:, :] = jnp.ones((ATTN_ONES_ROWS, seq), BF16)

    c_left = rb_ref[(half_buckets - 1) * ATTN_HEADS + h] * LOG2E
    c_right = rb_ref[(2 * half_buckets - 1) * ATTN_HEADS + h] * LOG2E

    def is_near(kc):
        return (kc - qi >= -1) & (kc - qi <= 1)

    def far_bias(kc):
        return jnp.where(is_near(kc), 0.0, jnp.where(kc < qi, c_left, c_right))

    q_t = qt_ref[...]
    row = lax.broadcasted_iota(jnp.int32, (ATTN_V_HEAD, tile), 0)
    zero = jnp.zeros_like(q_t)
    qs = (jnp.where(row < ATTN_HEAD_DIM, q_t, zero), jnp.where(row >= ATTN_HEAD_DIM, q_t, zero))

    m_ref[...] = jnp.full(m_ref.shape, -jnp.inf, F32)
    acc_ref[...] = jnp.zeros(acc_ref.shape, F32)

    def score_block(kc, t, r, s_out, m8, with_bias):
        rows = pl.ds(r * ATTN_ROWS_A, ATTN_ROWS_A)
        kt = k_ref[0, pl.ds(pl.multiple_of(kc * tile, tile) + r * ATTN_ROWS_A, ATTN_ROWS_A), :]
        sb = _dot(kt, qs[t])
        if with_bias:
            sb = sb + bias_ref[jnp.clip(kc - qi + 1, 0, 2), rows, :]
        s_out[t, rows, :] = sb
        for i in range(ATTN_ROWS_A // sub):
            m8 = jnp.maximum(m8, sb[i * sub:(i + 1) * sub, :])
        return m8

    def step_body(kc_next, s_next, kc, s_cur, m8s, with_bias):
        vt = vte_ref[:, pl.ds(pl.multiple_of(kc * tile, tile), tile)]
        shift = far_bias(kc)
        nblk = tile // ATTN_ROWS_A
        per = ATTN_ROWS_A // ATTN_ROWS_B
        m8s_next = []
        for t in range(2):
            m_old = m_ref[t]
            m_new = jnp.maximum(m_old, jnp.max(m8s[t], axis=0, keepdims=True) + shift)
            alpha = jnp.exp2(m_old - m_new)
            msub = m_new - shift
            m8 = jnp.full((sub, tile), -jnp.inf, F32)
            for r in range(nblk):
                m8 = score_block(kc_next, t, r, s_next, m8, with_bias)
                for rb in range(r * per, (r + 1) * per):
                    rows = pl.ds(rb * ATTN_ROWS_B, ATTN_ROWS_B)
                    p_ref[t, rows, :] = jnp.exp2(s_cur[t, rows, :] - msub).astype(BF16)
            acc_ref[t] = alpha * acc_ref[t] + _dot(vt, p_ref[t])
            m_ref[t] = m_new
            m8s_next.append(m8)
        return tuple(m8s_next)

    def step(kc_next, s_next, kc, s_cur, m8s):
        return lax.cond(is_near(kc_next),
                        lambda: step_body(kc_next, s_next, kc, s_cur, m8s, True),
                        lambda: step_body(kc_next, s_next, kc, s_cur, m8s, False))

    def chunk_pair(j, m8s):
        c0 = 2 * j
        m8s_odd = step(c0 + 1, sb_ref, c0, sa_ref, m8s)
        return step(jnp.minimum(c0 + 2, nk - 1), sa_ref, c0 + 1, sb_ref, m8s_odd)

    def first_scores(with_bias):
        def fn():
            m8s0 = []
            for t in range(2):
                m8 = jnp.full((sub, tile), -jnp.inf, F32)
                for r in range(tile // ATTN_ROWS_A):
                    m8 = score_block(0, t, r, sa_ref, m8, with_bias)
                m8s0.append(m8)
            return tuple(m8s0)
        return fn

    lax.fori_loop(0, nk // 2, chunk_pair,
                  lax.cond(is_near(0), first_scores(True), first_scores(False)))

    lv = lam_ref[...]
    lam = (jnp.exp(jnp.sum(lv[0:1] * lv[1:2], axis=-1, keepdims=True))
           - jnp.exp(jnp.sum(lv[2:3] * lv[3:4], axis=-1, keepdims=True)) + lam_init)
    o1 = acc_ref[0, 0:ATTN_V_HEAD, :] / acc_ref[0, ATTN_V_HEAD:ATTN_V_HEAD + 1, :]
    o2 = acc_ref[1, 0:ATTN_V_HEAD, :] / acc_ref[1, ATTN_V_HEAD:ATTN_V_HEAD + 1, :]
    o = (o1 - lam * o2).T
    o = _rms_rows(o, subln_ref[...]) * (1.0 - lam_init)
    o_ref[0] = (o * _silu(z_ref[0].astype(F32))).astype(o_ref.dtype)


def _attention(thr, rb_flat, qt, k3, vt, z3, lam_vec, subln, lam_init):
    bsz, seq, _ = k3.shape
    tile = min(ATTN_TILE, seq)
    nq = seq // tile
    assert tile > REL_MAX_DIST and seq % (2 * tile) == 0
    smem = pl.BlockSpec(memory_space=pltpu.SMEM)
    qspec = pl.BlockSpec((ATTN_V_HEAD, tile), lambda h, b, i: (h, b * nq + i))
    kspec = pl.BlockSpec((1, seq, ATTN_V_HEAD), lambda h, b, i: (b, 0, h))
    vspec = pl.BlockSpec((ATTN_V_HEAD, seq), lambda h, b, i: (h, b))
    zspec = pl.BlockSpec((1, tile, ATTN_V_HEAD), lambda h, b, i: (b, i, h))
    return pl.pallas_call(
        functools.partial(_attn_kernel, seq=seq, tile=tile, lam_init=lam_init),
        name="diff_attention",
        grid=(ATTN_HEADS, bsz, nq),
        in_specs=[smem, smem, qspec, kspec, vspec, zspec,
                  pl.BlockSpec(lam_vec.shape, lambda h, b, i: (0, 0)),
                  pl.BlockSpec(subln.shape, lambda h, b, i: (0, 0))],
        out_specs=zspec,
        out_shape=jax.ShapeDtypeStruct((bsz, seq, ATTN_V_DIM), BF16),
        scratch_shapes=[pltpu.VMEM((3, tile, tile), F32),
                        pltpu.VMEM((2, tile, tile), F32),
                        pltpu.VMEM((2, tile, tile), F32),
                        pltpu.VMEM((2, tile, tile), BF16),
                        pltpu.VMEM((2, 1, tile), F32),
                        pltpu.VMEM((2, ATTN_V_HEAD + ATTN_ONES_ROWS, tile), F32),
                        pltpu.VMEM((ATTN_V_HEAD + ATTN_ONES_ROWS, seq), BF16)],
        compiler_params=pltpu.CompilerParams(
            dimension_semantics=("arbitrary", "arbitrary", "arbitrary"),
            vmem_limit_bytes=VMEM_LIMIT),
    )(thr, rb_flat, qt, k3, vt, z3, lam_vec, subln)


def _attn_out_kernel(o_ref, w_ref, h_ref, out_ref):
    out_ref[...] = h_ref[...] + _dot(o_ref[...], w_ref[...])


def _attn_out_proj(o2, w, h2):
    t = h2.shape[0]
    tm = min(ROW_TILE, t)
    row = lambda n: pl.BlockSpec((tm, n), lambda i: (i, 0))
    return pl.pallas_call(
        _attn_out_kernel,
        name="attn_out_proj",
        grid=(t // tm,),
        in_specs=[row(ATTN_V_DIM), _resident(w.shape), row(D_MODEL)],
        out_specs=row(D_MODEL),
        out_shape=jax.ShapeDtypeStruct((t, D_MODEL), F32),
        compiler_params=pltpu.CompilerParams(dimension_semantics=("arbitrary",),
                                             vmem_limit_bytes=VMEM_LIMIT),
    )(o2, w, h2)


def _ssd_layer(h2, bsz, seq, nw, w_in, conv_w, conv_b, dt_bias, a_log, d_skip, norm_w, w_out):
    wb16 = w_in.astype(BF16)
    o_x = SSD_D_INNER
    o_b = 2 * SSD_D_INNER
    o_c = o_b + SSD_GN
    o_dt = o_c + SSD_GN
    wdt = wb16[:, o_dt:]
    wdc = jnp.concatenate([wdt, wdt], axis=1)
    perm = lambda a: a.reshape(2, SSD_GROUPS, SSD_HPG).transpose(1, 0, 2).reshape(2 * SSD_HEADS)
    wdr = wdt.T.reshape(2, SSD_GROUPS, SSD_HPG, D_MODEL).transpose(1, 0, 2, 3).reshape(
        2 * SSD_HEADS, D_MODEL)
    del o_b, o_c
    z2, x2, b2, c2, dtc, dtr = _ssd_in_proj(
        h2, seq, nw.reshape(1, -1), wb16[:, :o_x], wb16[:, o_x:o_dt], wdc, wdr,
        conv_w.astype(F32), conv_b.reshape(1, -1).astype(F32))

    flat = lambda a: a.reshape(2 * SSD_HEADS).astype(F32)
    biasc = jnp.tile(flat(dt_bias), 2).reshape(1, LANES)
    alogc = jnp.tile(flat(a_log), 2).reshape(1, LANES)
    biasr = jnp.broadcast_to(perm(flat(dt_bias))[:, None], (2 * SSD_HEADS, LANES))
    alogr = jnp.broadcast_to(perm(flat(a_log))[:, None], (2 * SSD_HEADS, LANES))
    d_exp = jnp.repeat(d_skip.astype(F32), SSD_HEADDIM).reshape(1, SSD_D_INNER)

    y3 = _ssd_scan(x2.reshape(bsz, seq, -1), b2.reshape(bsz, seq, -1), c2.reshape(bsz, seq, -1),
                   dtc.reshape(bsz, seq, LANES), dtr, biasc, alogc, biasr, alogr, d_exp)
    return _ssd_out_proj(y3.reshape(bsz * seq, -1), z2, norm_w.reshape(1, -1),
                         w_out.astype(BF16), h2)


def _t5_thresholds():
    nb = REL_BUCKETS // 2
    max_exact = nb // 2
    n = jnp.arange(REL_MAX_DIST + 1, dtype=jnp.int32)
    nf = jnp.maximum(n, 1).astype(F32)
    large = max_exact + (jnp.log(nf / max_exact) / math.log(REL_MAX_DIST / max_exact)
                         * (nb - max_exact)).astype(jnp.int32)
    bucket = jnp.where(n < max_exact, n, jnp.minimum(large, nb - 1))
    t = jnp.arange(nb, dtype=jnp.int32)
    return jnp.sum((bucket[None, :] < t[:, None]).astype(jnp.int32), axis=1)


def _attn_layer(h2, bsz, seq, nw, w_in, q_norm, k_norm, lam_vec, subln, w_out, thr, rb_flat,
                ones_bd, layer_idx):
    wb16 = w_in.astype(BF16)
    reps = ATTN_QK_DIM // ATTN_HEAD_DIM
    tm = min(ROW_TILE, bsz * seq)
    qn = jnp.tile(q_norm.astype(F32), reps) * (ATTN_HEAD_DIM ** -0.5 * LOG2E)
    qn_cols = jnp.broadcast_to(qn[:, None], (ATTN_QK_DIM, tm))
    kn = jnp.tile(k_norm.astype(F32), reps).reshape(1, -1)
    qt, k2, vt, z2 = _attn_in_proj(
        h2, nw.reshape(1, -1), wb16[:, :ATTN_QK_DIM].T, wb16[:, ATTN_QK_DIM:2 * ATTN_QK_DIM],
        wb16[:, 2 * ATTN_QK_DIM:2 * ATTN_QK_DIM + ATTN_V_DIM].T,
        wb16[:, 2 * ATTN_QK_DIM + ATTN_V_DIM:], qn_cols, kn, ones_bd)
    lam_init = 0.8 - 0.6 * math.exp(-0.3 * layer_idx)
    r3 = lambda a: a.reshape(bsz, seq, -1)
    o3 = _attention(thr, rb_flat, qt, r3(k2), vt, r3(z2), lam_vec.astype(F32),
                    subln.reshape(1, -1).astype(F32), lam_init)
    return _attn_out_proj(o3.reshape(bsz * seq, -1), w_out.astype(BF16), h2)


def kernel(x, norm_w, ssd_w_in, ssd_conv_w, ssd_conv_b, ssd_dt_bias, ssd_a_log, ssd_d, ssd_norm_w, ssd_w_out, attn_w_in, attn_q_norm, attn_k_norm, attn_lambda, attn_subln, attn_w_out, rel_bias):
    bsz, seq, _ = x.shape
    depth = norm_w.shape[0]
    h2 = x.reshape(bsz * seq, D_MODEL)
    thr = _t5_thresholds()
    rb_flat = rel_bias.astype(F32).reshape(-1)
    blk = jnp.arange(ATTN_QK_DIM, dtype=jnp.int32) // ATTN_HEAD_DIM
    ones_bd = (blk[:, None] == blk[None, :]).astype(BF16)
    for i in range(depth):
        j = i // 2
        if i % 2 == 0:
            h2 = _ssd_layer(h2, bsz, seq, norm_w[i], ssd_w_in[j], ssd_conv_w[j], ssd_conv_b[j],
                            ssd_dt_bias[j], ssd_a_log[j], ssd_d[j], ssd_norm_w[j], ssd_w_out[j])
        else:
            h2 = _attn_layer(h2, bsz, seq, norm_w[i], attn_w_in[j], attn_q_norm[j], attn_k_norm[j],
                             attn_lambda[j], attn_subln[j], attn_w_out[j], thr, rb_flat, ones_bd, i)
    return h2.reshape(bsz, seq, D_MODEL)
```

```python
import functools
import math

import jax
import jax.numpy as jnp
from jax import lax
from jax.experimental import pallas as pl
from jax.experimental.pallas import tpu as pltpu

F32 = jnp.float32
BF16 = jnp.bfloat16

RMS_EPS = 1e-6

D_MODEL = 1024
SSD_D_INNER = 2048
SSD_HEADDIM = 64
SSD_HEADS = 32
SSD_D_STATE = 128
SSD_GROUPS = 8
SSD_HPG = 4
SSD_CONV = 5
SSD_CHUNK = 128
SSD_GN = SSD_GROUPS * SSD_D_STATE
ATTN_HEADS = 8
ATTN_HEAD_DIM = 64
ATTN_V_HEAD = 128
ATTN_QK_DIM = 1024
ATTN_V_DIM = 1024
REL_BUCKETS = 32
REL_MAX_DIST = 128

V7X_VMEM_BYTES = 64 * 1024 * 1024
VMEM_LIMIT = V7X_VMEM_BYTES - 8 * 1024 * 1024
LANES = 128
BF16_SUBLANES = 16

ROW_TILE = 512
HALO = BF16_SUBLANES
CONV_COLS = 512
ATTN_TILE = 512
ATTN_ROWS_A = 128
ATTN_ROWS_B = 64
ATTN_ONES_ROWS = BF16_SUBLANES
LOG2E = 1.4426950408889634
GS = SSD_HPG * SSD_HEADDIM


def _resident(shape):
    nd = len(shape)
    return pl.BlockSpec(shape, lambda *_: (0,) * nd, pipeline_mode=pl.Buffered(1))


def _silu(x):
    return x * (1.0 / (1.0 + jnp.exp(-x)))


def _softplus(x):
    return jnp.maximum(x, 0.0) + jnp.log(1.0 + jnp.exp(-jnp.abs(x)))


def _rms_rows(x, w_row):
    ms = jnp.mean(x * x, axis=-1, keepdims=True)
    return x * lax.rsqrt(ms + RMS_EPS) * w_row


def _dot(a, b):
    return jnp.dot(a, b, preferred_element_type=F32)


def _dot_nt(a, b):
    return lax.dot_general(a, b, (((1,), (1,)), ((), ())), preferred_element_type=F32)


def _split3(x):
    hi = x.astype(BF16)
    r1 = x - hi.astype(F32)
    mid = r1.astype(BF16)
    lo = (r1 - mid.astype(F32)).astype(BF16)
    return hi, mid, lo


def _ssd_in_kernel(h_ref, hp_ref, hn_ref, nw_ref, wz_ref, wxbc_ref, wdc_ref, wdr_ref,
                   cw_ref, cb_ref, biasc_ref, alogc_ref, biasr_ref, alogr_ref,
                   z_ref, x_ref, b_ref, c_ref, csc_ref, dtr_ref, csr_ref, pre_ref, *, seq):
    i = pl.program_id(0)
    tm = h_ref.shape[0]
    q = SSD_CHUNK
    nw = nw_ref[...]
    at_start = (i * tm) % seq == 0
    at_end = ((i + 1) * tm) % seq == 0
    xn = _rms_rows(h_ref[...], nw).astype(BF16)
    xp = (_rms_rows(hp_ref[...], nw) * jnp.where(at_start, 0.0, 1.0)).astype(BF16)
    xq = (_rms_rows(hn_ref[...], nw) * jnp.where(at_end, 0.0, 1.0)).astype(BF16)
    ext = jnp.concatenate([xp, xn, xq], axis=0)

    z_ref[...] = _dot(xn, wz_ref[...]).astype(BF16)

    rows = lax.broadcasted_iota(jnp.int32, (q, q), 0)
    cols = lax.broadcasted_iota(jnp.int32, (q, q), 1)
    lower_b = (cols <= rows).astype(BF16)
    upper_f = (rows <= cols).astype(F32)
    lower_f = (rows >= cols).astype(F32)
    a_cols = _softplus(_dot(xn, wdc_ref[...]) + biasc_ref[...]) * -jnp.exp(alogc_ref[...])
    bwd_lane = (lax.broadcasted_iota(jnp.int32, (q, LANES), 1) & (SSD_HEADS)) != 0
    dt_rows = _softplus(_dot_nt(wdr_ref[...], xn) + biasr_ref[...])
    a_rows = dt_rows * -jnp.exp(alogr_ref[...])
    dtr_ref[...] = dt_rows
    bwd_row = (lax.broadcasted_iota(jnp.int32, (2 * SSD_HEADS, q), 0) & SSD_HPG) != 0
    for j in range(tm // q):
        blk = a_cols[j * q:(j + 1) * q, :]
        hi, mid, lo = _split3(blk)
        r3 = _dot(lower_b, jnp.concatenate([hi, mid, lo], axis=1))
        pre = r3[:, :LANES] + r3[:, LANES:2 * LANES] + r3[:, 2 * LANES:]
        suf = pre[q - 1:q, :] - pre + blk
        csc_ref[j * q:(j + 1) * q, :] = jnp.where(bwd_lane, suf, pre)
        rblk = a_rows[:, j * q:(j + 1) * q]
        pre_r = jnp.dot(rblk, upper_f, preferred_element_type=F32, precision=lax.Precision.HIGHEST)
        suf_r = jnp.dot(rblk, lower_f, preferred_element_type=F32, precision=lax.Precision.HIGHEST)
        csr_ref[:, j * q:(j + 1) * q] = jnp.where(bwd_row, suf_r, pre_r)

    half = SSD_CONV // 2
    outs = ((x_ref, 0, SSD_D_INNER), (b_ref, SSD_D_INNER, SSD_GN),
            (c_ref, SSD_D_INNER + SSD_GN, SSD_GN))
    for o_ref, base, width in outs:
        for j in range(width // CONV_COLS):
            lo = base + j * CONV_COLS
            pre_ref[...] = _dot(ext, wxbc_ref[:, lo:lo + CONV_COLS])
            w = cw_ref[:, lo:lo + CONV_COLS]
            acc = cb_ref[:, lo:lo + CONV_COLS]
            for k in range(SSD_CONV):
                off = HALO - half + k
                acc = acc + pre_ref[off:off + tm, :] * w[k:k + 1, :]
            o_ref[:, j * CONV_COLS:(j + 1) * CONV_COLS] = _silu(acc).astype(BF16)


def _ssd_in_proj(h2, seq, nw, wz, wxbc, wdc, wdr, conv_w, conv_b, biasc, alogc, biasr, alogr):
    t = h2.shape[0]
    tm = min(ROW_TILE, seq)
    assert seq % tm == 0 and tm % HALO == 0 and tm % SSD_CHUNK == 0 and biasr.shape[1] == tm
    per = tm // HALO
    last = t // HALO - 1
    row = lambda n: pl.BlockSpec((tm, n), lambda i: (i, 0))
    return pl.pallas_call(
        functools.partial(_ssd_in_kernel, seq=seq),
        name="ssd_in_proj",
        grid=(t // tm,),
        in_specs=[row(D_MODEL),
                  pl.BlockSpec((HALO, D_MODEL), lambda i: (jnp.maximum(i * per - 1, 0), 0)),
                  pl.BlockSpec((HALO, D_MODEL), lambda i: (jnp.minimum((i + 1) * per, last), 0)),
                  _resident(nw.shape), _resident(wz.shape), _resident(wxbc.shape),
                  _resident(wdc.shape), _resident(wdr.shape), _resident(conv_w.shape),
                  _resident(conv_b.shape), _resident(biasc.shape), _resident(alogc.shape),
                  _resident(biasr.shape), _resident(alogr.shape)],
        out_specs=[row(SSD_D_INNER), row(SSD_D_INNER), row(SSD_GN), row(SSD_GN), row(LANES),
                   pl.BlockSpec((2 * SSD_HEADS, tm), lambda i: (0, i)),
                   pl.BlockSpec((2 * SSD_HEADS, tm), lambda i: (0, i))],
        out_shape=[jax.ShapeDtypeStruct((t, SSD_D_INNER), BF16),
                   jax.ShapeDtypeStruct((t, SSD_D_INNER), BF16),
                   jax.ShapeDtypeStruct((t, SSD_GN), BF16),
                   jax.ShapeDtypeStruct((t, SSD_GN), BF16),
                   jax.ShapeDtypeStruct((t, LANES), F32),
                   jax.ShapeDtypeStruct((2 * SSD_HEADS, t), F32),
                   jax.ShapeDtypeStruct((2 * SSD_HEADS, t), F32)],
        scratch_shapes=[pltpu.VMEM((tm + 2 * HALO, CONV_COLS), F32)],
        compiler_params=pltpu.CompilerParams(dimension_semantics=("arbitrary",),
                                             vmem_limit_bytes=VMEM_LIMIT),
    )(h2, h2, h2, nw, wz, wxbc, wdc, wdr, conv_w, conv_b, biasc, alogc, biasr, alogr)


def _ssd_scan_kernel(x_ref, b_ref, c_ref, csc_ref, dtr_ref, csr_ref, d_ref,
                     y_ref, acc_ref, stf_ref, stb_ref, e_ref, *, seq):
    g = pl.program_id(1)
    q = SSD_CHUNK
    nchunk = seq // q

    rows = lax.broadcasted_iota(jnp.int32, (q, q), 0)
    cols = lax.broadcasted_iota(jnp.int32, (q, q), 1)
    lane = lax.broadcasted_iota(jnp.int32, (q, LANES), 1)
    lane_gs = lax.broadcasted_iota(jnp.int32, (q, GS), 1)

    ek = lax.broadcasted_iota(jnp.int32, (2 * LANES, SSD_HPG * q), 0)
    en = lax.broadcasted_iota(jnp.int32, (2 * LANES, SSD_HPG * q), 1) >> 7
    for d in range(2):
        col = d * SSD_HEADS + g * SSD_HPG + en
        e_ref[d] = ((ek == col) | (ek == col + 64) | (ek == col + LANES)).astype(BF16)

    d_row = d_ref[...]

    def front(c, rev):
        s0 = pl.multiple_of(c * q, q)
        xs_b = x_ref[0, pl.ds(s0, q), :]
        bm_b = b_ref[0, pl.ds(s0, q), :]
        cm_b = c_ref[0, pl.ds(s0, q), :]
        gmat = _dot_nt(cm_b, bm_b)
        bm_t = bm_b.astype(F32).T
        keep = (cols >= rows) if rev else (cols <= rows)
        grp_rows = pl.ds(pl.multiple_of(g * 8, 8), 8)
        return dict(xs_b=xs_b, cm_b=cm_b, gmat=gmat, bm_t=bm_t, keep=keep,
                    dt_rows=dtr_ref[grp_rows, pl.ds(s0, q)],
                    cs_rows=csr_ref[grp_rows, pl.ds(s0, q)],
                    cs_cols=csc_ref[0, pl.ds(s0, q), :])

    def middle(f, rev):
        d = 1 if rev else 0
        hi, mid, lo = _split3(f["cs_cols"])
        lhs = jnp.concatenate([jnp.where(lane < 64, hi, mid), lo], axis=1)
        bc = _dot(lhs, e_ref[d])
        csx = jnp.concatenate(
            [jnp.where(lane < 64, bc[:, 0:q], bc[:, q:2 * q]),
             jnp.where(lane < 64, bc[:, 2 * q:3 * q], bc[:, 3 * q:4 * q])], axis=1)
        mmats, bts, xms = [], [], []
        for r in range(SSD_HPG):
            row = d * SSD_HPG + r
            cs_r = f["cs_rows"][row:row + 1, :]
            dt_r = f["dt_rows"][row:row + 1, :]
            end_r = cs_r[:, 0:1] if rev else cs_r[:, q - 1:q]
            seg = bc[:, r * q:(r + 1) * q] - cs_r
            lmat = jnp.exp(jnp.minimum(seg, 0.0))
            mmats.append(jnp.where(f["keep"], f["gmat"] * lmat * dt_r, 0.0).astype(BF16))
            xms.append(jnp.where((lane_gs >> 6) == r, f["xs_b"], jnp.zeros_like(f["xs_b"])))
            w_r = jnp.exp(end_r - cs_r) * dt_r
            bts.append((f["bm_t"] * w_r).astype(BF16))
        pairs = [(jnp.concatenate(mmats[r:r + 2], axis=1), jnp.concatenate(bts[r:r + 2], axis=1),
                  jnp.concatenate(xms[r:r + 2], axis=0)) for r in range(0, SSD_HPG, 2)]
        return dict(cm_b=f["cm_b"], xs_b=f["xs_b"], decay=jnp.exp(csx),
                    end_decay=jnp.exp(csx[0:1, :] if rev else csx[q - 1:q, :]), pairs=pairs)

    def back(m, rev):
        st_ref = stb_ref if rev else stf_ref
        state = st_ref[...]
        y = _dot(m["cm_b"], state.astype(BF16)) * m["decay"]
        dstate = jnp.zeros((SSD_D_STATE, GS), F32)
        for mm, bt, xpair in m["pairs"]:
            y = y + _dot(mm, xpair)
            dstate = dstate + _dot(bt, xpair)
        st_ref[...] = state * m["end_decay"] + dstate
        return y

    def step(i, finish):
        work = ((2 * i, False), (nchunk - 1 - 2 * i, True),
                (2 * i + 1, False), (nchunk - 2 - 2 * i, True))
        fronts = [front(c, rev) for c, rev in work]
        mids = [middle(f, rev) for f, (_, rev) in zip(fronts, work)]
        for m, (c, rev) in zip(mids, work):
            y = back(m, rev)
            if not rev:
                y = y + m["xs_b"].astype(F32) * d_row
            rs = pl.ds(pl.multiple_of(c * q, q), q)
            if finish:
                y_ref[0, rs, :] = (acc_ref[rs, :] + y).astype(y_ref.dtype)
            else:
                acc_ref[rs, :] = y

    stf_ref[...] = jnp.zeros_like(stf_ref)
    stb_ref[...] = jnp.zeros_like(stb_ref)

    def first_half(i, carry):
        step(i, False)
        return carry

    def second_half(i, carry):
        step(i, True)
        return carry

    lax.fori_loop(0, nchunk // 4, first_half, 0)
    lax.fori_loop(nchunk // 4, nchunk // 2, second_half, 0)


def _ssd_scan(x3, b3, c3, csc3, dtr, csr, d_exp):
    bsz, seq, _ = x3.shape
    assert seq % (4 * SSD_CHUNK) == 0
    grp = lambda n: pl.BlockSpec((1, seq, n), lambda b, g: (b, 0, g))
    heads_rows = pl.BlockSpec((2 * SSD_HEADS, seq), lambda b, g: (0, b))
    return pl.pallas_call(
        functools.partial(_ssd_scan_kernel, seq=seq),
        name="ssd_scan",
        grid=(bsz, SSD_GROUPS),
        in_specs=[grp(GS), grp(SSD_D_STATE), grp(SSD_D_STATE),
                  pl.BlockSpec((1, seq, LANES), lambda b, g: (b, 0, 0)),
                  heads_rows, heads_rows,
                  pl.BlockSpec((1, GS), lambda b, g: (0, g))],
        out_specs=pl.BlockSpec((1, seq, GS), lambda b, g: (b, 0, g)),
        out_shape=jax.ShapeDtypeStruct((bsz, seq, SSD_D_INNER), BF16),
        scratch_shapes=[pltpu.VMEM((seq, GS), F32),
                        pltpu.VMEM((SSD_D_STATE, GS), F32),
                        pltpu.VMEM((SSD_D_STATE, GS), F32),
                        pltpu.VMEM((2, 2 * LANES, SSD_HPG * SSD_CHUNK), BF16)],
        compiler_params=pltpu.CompilerParams(dimension_semantics=("arbitrary", "arbitrary"),
                                             vmem_limit_bytes=VMEM_LIMIT),
    )(x3, b3, c3, csc3, dtr, csr, d_exp)


def _ssd_out_kernel(y_ref, z_ref, nw_ref, w_ref, h_ref, o_ref):
    u = y_ref[...].astype(F32) * _silu(z_ref[...].astype(F32))
    un = _rms_rows(u, nw_ref[...]).astype(BF16)
    o_ref[...] = h_ref[...] + _dot(un, w_ref[...])


def _ssd_out_proj(y2, z2, nw, w, h2):
    t = h2.shape[0]
    tm = min(ROW_TILE, t)
    row = lambda n: pl.BlockSpec((tm, n), lambda i: (i, 0))
    return pl.pallas_call(
        _ssd_out_kernel,
        name="ssd_out_proj",
        grid=(t // tm,),
        in_specs=[row(SSD_D_INNER), row(SSD_D_INNER), _resident(nw.shape), _resident(w.shape),
                  row(D_MODEL)],
        out_specs=row(D_MODEL),
        out_shape=jax.ShapeDtypeStruct((t, D_MODEL), F32),
        compiler_params=pltpu.CompilerParams(dimension_semantics=("arbitrary",),
                                             vmem_limit_bytes=VMEM_LIMIT),
    )(y2, z2, nw, w, h2)


def _attn_in_kernel(h_ref, nw_ref, wqt_ref, wk_ref, wvt_ref, wz_ref, qn_ref, kn_ref, ones_ref,
                    qt_ref, k_ref, vt_ref, z_ref):
    xn = _rms_rows(h_ref[...], nw_ref[...]).astype(BF16)
    inv_d = 1.0 / ATTN_HEAD_DIM

    tm = h_ref.shape[0]
    yq = _dot_nt(wqt_ref[...], xn).reshape(-1, ATTN_HEAD_DIM, tm)
    ssq = jnp.sum(yq * yq, axis=1, keepdims=True)
    qn = (yq * lax.rsqrt(ssq * inv_d + RMS_EPS)).reshape(ATTN_QK_DIM, tm)
    qt_ref[...] = (qn * qn_ref[...]).astype(BF16)

    yk = _dot(xn, wk_ref[...])
    ssk = _dot((yk * yk).astype(BF16), ones_ref[...])
    k_ref[...] = (yk * lax.rsqrt(ssk * inv_d + RMS_EPS) * kn_ref[...]).astype(BF16)

    vt_ref[...] = _dot_nt(wvt_ref[...], xn).astype(BF16)
    z_ref[...] = _dot(xn, wz_ref[...]).astype(BF16)


def _attn_in_proj(h2, nw, wqt, wk, wvt, wz, qn_cols, kn, ones_bd):
    t = h2.shape[0]
    tm = qn_cols.shape[1]
    row = lambda n: pl.BlockSpec((tm, n), lambda i: (i, 0))
    col = lambda n: pl.BlockSpec((n, tm), lambda i: (0, i))
    return pl.pallas_call(
        _attn_in_kernel,
        name="attn_in_proj",
        grid=(t // tm,),
        in_specs=[row(D_MODEL)] + [_resident(a.shape)
                                   for a in (nw, wqt, wk, wvt, wz, qn_cols, kn, ones_bd)],
        out_specs=[col(ATTN_QK_DIM), row(ATTN_QK_DIM), col(ATTN_V_DIM), row(ATTN_V_DIM)],
        out_shape=[jax.ShapeDtypeStruct((ATTN_QK_DIM, t), BF16),
                   jax.ShapeDtypeStruct((t, ATTN_QK_DIM), BF16),
                   jax.ShapeDtypeStruct((ATTN_V_DIM, t), BF16),
                   jax.ShapeDtypeStruct((t, ATTN_V_DIM), BF16)],
        compiler_params=pltpu.CompilerParams(dimension_semantics=("arbitrary",),
                                             vmem_limit_bytes=VMEM_LIMIT),
    )(h2, nw, wqt, wk, wvt, wz, qn_cols, kn, ones_bd)


def _attn_kernel(thr_ref, rb_ref, qt_ref, k_ref, vt_ref, z_ref, lam_ref, subln_ref, o_ref,
                 bias_ref, sa_ref, sb_ref, p_ref, m_ref, acc_ref, vte_ref, *, seq, tile, lam_init):
    h = pl.program_id(0)
    b = pl.program_id(1)
    qi = pl.program_id(2)
    nk = seq // tile
    half_buckets = REL_BUCKETS // 2
    sub = 8

    @pl.when((b == 0) & (qi == 0))
    def _build_bias():
        rel0 = (lax.broadcasted_iota(jnp.int32, (tile, tile), 0)
                - lax.broadcasted_iota(jnp.int32, (tile, tile), 1))
        for idx in range(3):
            rel = rel0 + (idx - 1) * tile
            n = jnp.abs(rel)
            neg = jnp.full((tile, tile), rb_ref[h], F32)
            pos = jnp.full((tile, tile), rb_ref[half_buckets * ATTN_HEADS + h], F32)
            for t in range(1, half_buckets):
                ge = n >= thr_ref[t]
                neg = jnp.where(ge, rb_ref[t * ATTN_HEADS + h], neg)
                pos = jnp.where(ge, rb_ref[(half_buckets + t) * ATTN_HEADS + h], pos)
            bias_ref[idx] = jnp.where(rel > 0, pos, neg) * LOG2E

    @pl.when(qi == 0)
    def _extend_v():
        vte_ref[0:ATTN_V_HEAD, :] = vt_ref[...]
        vte_ref[ATTN_V_HEAD:, :] = jnp.ones((ATTN_ONES_ROWS, seq), BF16)

    c_left = rb_ref[(half_buckets - 1) * ATTN_HEADS + h] * LOG2E
    c_right = rb_ref[(2 * half_buckets - 1) * ATTN_HEADS + h] * LOG2E

    def is_near(kc):
        return (kc - qi >= -1) & (kc - qi <= 1)

    def far_bias(kc):
        return jnp.where(is_near(kc), 0.0, jnp.where(kc < qi, c_left, c_right))

    q_t = qt_ref[...]
    row = lax.broadcasted_iota(jnp.int32, (ATTN_V_HEAD, tile), 0)
    zero = jnp.zeros_like(q_t)
    qs = (jnp.where(row < ATTN_HEAD_DIM, q_t, zero), jnp.where(row >= ATTN_HEAD_DIM, q_t, zero))

    m_ref[...] = jnp.full(m_ref.shape, -jnp.inf, F32)
    acc_ref[...] = jnp.zeros(acc_ref.shape, F32)

    def score_block(kc, t, r, s_out, m8, with_bias):
        rows = pl.ds(r * ATTN_ROWS_A, ATTN_ROWS_A)
        kt = k_ref[0, pl.ds(pl.multiple_of(kc * tile, tile) + r * ATTN_ROWS_A, ATTN_ROWS_A), :]
        sb = _dot(kt, qs[t])
        if with_bias:
            sb = sb + bias_ref[jnp.clip(kc - qi + 1, 0, 2), rows, :]
        s_out[t, rows, :] = sb
        for i in range(ATTN_ROWS_A // sub):
            m8 = jnp.maximum(m8, sb[i * sub:(i + 1) * sub, :])
        return m8

    def step_body(kc_next, s_next, kc, s_cur, m8s, with_bias):
        vt = vte_ref[:, pl.ds(pl.multiple_of(kc * tile, tile), tile)]
        shift = far_bias(kc)
        nblk = tile // ATTN_ROWS_A
        per = ATTN_ROWS_A // ATTN_ROWS_B
        m8s_next = []
        for t in range(2):
            m_old = m_ref[t]
            m_new = jnp.maximum(m_old, jnp.max(m8s[t], axis=0, keepdims=True) + shift)
            alpha = jnp.exp2(m_old - m_new)
            msub = m_new - shift
            m8 = jnp.full((sub, tile), -jnp.inf, F32)
            for r in range(nblk):
                m8 = score_block(kc_next, t, r, s_next, m8, with_bias)
                for rb in range(r * per, (r + 1) * per):
                    rows = pl.ds(rb * ATTN_ROWS_B, ATTN_ROWS_B)
                    p_ref[t, rows, :] = jnp.exp2(s_cur[t, rows, :] - msub).astype(BF16)
            acc_ref[t] = alpha * acc_ref[t] + _dot(vt, p_ref[t])
            m_ref[t] = m_new
            m8s_next.append(m8)
        return tuple(m8s_next)

    def step(kc_next, s_next, kc, s_cur, m8s):
        return lax.cond(is_near(kc_next),
                        lambda: step_body(kc_next, s_next, kc, s_cur, m8s, True),
                        lambda: step_body(kc_next, s_next, kc, s_cur, m8s, False))

    def chunk_pair(j, m8s):
        c0 = 2 * j
        m8s_odd = step(c0 + 1, sb_ref, c0, sa_ref, m8s)
        return step(jnp.minimum(c0 + 2, nk - 1), sa_ref, c0 + 1, sb_ref, m8s_odd)

    def first_scores(with_bias):
        def fn():
            m8s0 = []
            for t in range(2):
                m8 = jnp.full((sub, tile), -jnp.inf, F32)
                for r in range(tile // ATTN_ROWS_A):
                    m8 = score_block(0, t, r, sa_ref, m8, with_bias)
                m8s0.append(m8)
            return tuple(m8s0)
        return fn

    lax.fori_loop(0, nk // 2, chunk_pair,
                  lax.cond(is_near(0), first_scores(True), first_scores(False)))

    lv = lam_ref[...]
    lam = (jnp.exp(jnp.sum(lv[0:1] * lv[1:2], axis=-1, keepdims=True))
           - jnp.exp(jnp.sum(lv[2:3] * lv[3:4], axis=-1, keepdims=True)) + lam_init)
    o1 = acc_ref[0, 0:ATTN_V_HEAD, :] / acc_ref[0, ATTN_V_HEAD:ATTN_V_HEAD + 1, :]
    o2 = acc_ref[1, 0:ATTN_V_HEAD, :] / acc_ref[1, ATTN_V_HEAD:ATTN_V_HEAD + 1, :]
    o = (o1 - lam * o2).T
    o = _rms_rows(o, subln_ref[...]) * (1.0 - lam_init)
    o_ref[0] = (o * _silu(z_ref[0].astype(F32))).astype(o_ref.dtype)


def _attention(thr, rb_flat, qt, k3, vt, z3, lam_vec, subln, lam_init):
    bsz, seq, _ = k3.shape
    tile = min(ATTN_TILE, seq)
    nq = seq // tile
    assert tile > REL_MAX_DIST and seq % (2 * tile) == 0
    smem = pl.BlockSpec(memory_space=pltpu.SMEM)
    qspec = pl.BlockSpec((ATTN_V_HEAD, tile), lambda h, b, i: (h, b * nq + i))
    kspec = pl.BlockSpec((1, seq, ATTN_V_HEAD), lambda h, b, i: (b, 0, h))
    vspec = pl.BlockSpec((ATTN_V_HEAD, seq), lambda h, b, i: (h, b))
    zspec = pl.BlockSpec((1, tile, ATTN_V_HEAD), lambda h, b, i: (b, i, h))
    return pl.pallas_call(
        functools.partial(_attn_kernel, seq=seq, tile=tile, lam_init=lam_init),
        name="diff_attention",
        grid=(ATTN_HEADS, bsz, nq),
        in_specs=[smem, smem, qspec, kspec, vspec, zspec,
                  pl.BlockSpec(lam_vec.shape, lambda h, b, i: (0, 0)),
                  pl.BlockSpec(subln.shape, lambda h, b, i: (0, 0))],
        out_specs=zspec,
        out_shape=jax.ShapeDtypeStruct((bsz, seq, ATTN_V_DIM), BF16),
        scratch_shapes=[pltpu.VMEM((3, tile, tile), F32),
                        pltpu.VMEM((2, tile, tile), F32),
                        pltpu.VMEM((2, tile, tile), F32),
                        pltpu.VMEM((2, tile, tile), BF16),
                        pltpu.VMEM((2, 1, tile), F32),
                        pltpu.VMEM((2, ATTN_V_HEAD + ATTN_ONES_ROWS, tile), F32),
                        pltpu.VMEM((ATTN_V_HEAD + ATTN_ONES_ROWS, seq), BF16)],
        compiler_params=pltpu.CompilerParams(
            dimension_semantics=("arbitrary", "arbitrary", "arbitrary"),
            vmem_limit_bytes=VMEM_LIMIT),
    )(thr, rb_flat, qt, k3, vt, z3, lam_vec, subln)


def _attn_out_kernel(o_ref, w_ref, h_ref, out_ref):
    out_ref[...] = h_ref[...] + _dot(o_ref[...], w_ref[...])


def _attn_out_proj(o2, w, h2):
    t = h2.shape[0]
    tm = min(ROW_TILE, t)
    row = lambda n: pl.BlockSpec((tm, n), lambda i: (i, 0))
    return pl.pallas_call(
        _attn_out_kernel,
        name="attn_out_proj",
        grid=(t // tm,),
        in_specs=[row(ATTN_V_DIM), _resident(w.shape), row(D_MODEL)],
        out_specs=row(D_MODEL),
        out_shape=jax.ShapeDtypeStruct((t, D_MODEL), F32),
        compiler_params=pltpu.CompilerParams(dimension_semantics=("arbitrary",),
                                             vmem_limit_bytes=VMEM_LIMIT),
    )(o2, w, h2)


def _ssd_layer(h2, bsz, seq, nw, w_in, conv_w, conv_b, dt_bias, a_log, d_skip, norm_w, w_out):
    wb16 = w_in.astype(BF16)
    o_x = SSD_D_INNER
    o_b = 2 * SSD_D_INNER
    o_c = o_b + SSD_GN
    o_dt = o_c + SSD_GN
    wdt = wb16[:, o_dt:]
    wdc = jnp.concatenate([wdt, wdt], axis=1)
    perm = lambda a: a.reshape(2, SSD_GROUPS, SSD_HPG).transpose(1, 0, 2).reshape(2 * SSD_HEADS)
    wdr = wdt.T.reshape(2, SSD_GROUPS, SSD_HPG, D_MODEL).transpose(1, 0, 2, 3).reshape(
        2 * SSD_HEADS, D_MODEL)
    del o_b, o_c
    tm = min(ROW_TILE, seq)
    flat = lambda a: a.reshape(2 * SSD_HEADS).astype(F32)
    biasc = jnp.tile(flat(dt_bias), 2).reshape(1, LANES)
    alogc = jnp.tile(flat(a_log), 2).reshape(1, LANES)
    biasr = jnp.broadcast_to(perm(flat(dt_bias))[:, None], (2 * SSD_HEADS, tm))
    alogr = jnp.broadcast_to(perm(flat(a_log))[:, None], (2 * SSD_HEADS, tm))
    d_exp = jnp.repeat(d_skip.astype(F32), SSD_HEADDIM).reshape(1, SSD_D_INNER)
    z2, x2, b2, c2, csc, dtr, csr = _ssd_in_proj(
        h2, seq, nw.reshape(1, -1), wb16[:, :o_x], wb16[:, o_x:o_dt], wdc, wdr,
        conv_w.astype(F32), conv_b.reshape(1, -1).astype(F32), biasc, alogc, biasr, alogr)

    y3 = _ssd_scan(x2.reshape(bsz, seq, -1), b2.reshape(bsz, seq, -1), c2.reshape(bsz, seq, -1),
                   csc.reshape(bsz, seq, LANES), dtr, csr, d_exp)
    return _ssd_out_proj(y3.reshape(bsz * seq, -1), z2, norm_w.reshape(1, -1),
                         w_out.astype(BF16), h2)


def _t5_thresholds():
    nb = REL_BUCKETS // 2
    max_exact = nb // 2
    n = jnp.arange(REL_MAX_DIST + 1, dtype=jnp.int32)
    nf = jnp.maximum(n, 1).astype(F32)
    large = max_exact + (jnp.log(nf / max_exact) / math.log(REL_MAX_DIST / max_exact)
                         * (nb - max_exact)).astype(jnp.int32)
    bucket = jnp.where(n < max_exact, n, jnp.minimum(large, nb - 1))
    t = jnp.arange(nb, dtype=jnp.int32)
    return jnp.sum((bucket[None, :] < t[:, None]).astype(jnp.int32), axis=1)


def _attn_layer(h2, bsz, seq, nw, w_in, q_norm, k_norm, lam_vec, subln, w_out, thr, rb_flat,
                ones_bd, layer_idx):
    wb16 = w_in.astype(BF16)
    reps = ATTN_QK_DIM // ATTN_HEAD_DIM
    tm = min(ROW_TILE, bsz * seq)
    qn = jnp.tile(q_norm.astype(F32), reps) * (ATTN_HEAD_DIM ** -0.5 * LOG2E)
    qn_cols = jnp.broadcast_to(qn[:, None], (ATTN_QK_DIM, tm))
    kn = jnp.tile(k_norm.astype(F32), reps).reshape(1, -1)
    qt, k2, vt, z2 = _attn_in_proj(
        h2, nw.reshape(1, -1), wb16[:, :ATTN_QK_DIM].T, wb16[:, ATTN_QK_DIM:2 * ATTN_QK_DIM],
        wb16[:, 2 * ATTN_QK_DIM:2 * ATTN_QK_DIM + ATTN_V_DIM].T,
        wb16[:, 2 * ATTN_QK_DIM + ATTN_V_DIM:], qn_cols, kn, ones_bd)
    lam_init = 0.8 - 0.6 * math.exp(-0.3 * layer_idx)
    r3 = lambda a: a.reshape(bsz, seq, -1)
    o3 = _attention(thr, rb_flat, qt, r3(k2), vt, r3(z2), lam_vec.astype(F32),
                    subln.reshape(1, -1).astype(F32), lam_init)
    return _attn_out_proj(o3.reshape(bsz * seq, -1), w_out.astype(BF16), h2)


def kernel(x, norm_w, ssd_w_in, ssd_conv_w, ssd_conv_b, ssd_dt_bias, ssd_a_log, ssd_d, ssd_norm_w, ssd_w_out, attn_w_in, attn_q_norm, attn_k_norm, attn_lambda, attn_subln, attn_w_out, rel_bias):
    bsz, seq, _ = x.shape
    depth = norm_w.shape[0]
    h2 = x.reshape(bsz * seq, D_MODEL)
    thr = _t5_thresholds()
    rb_flat = rel_bias.astype(F32).reshape(-1)
    blk = jnp.arange(ATTN_QK_DIM, dtype=jnp.int32) // ATTN_HEAD_DIM
    ones_bd = (blk[:, None] == blk[None, :]).astype(BF16)
    for i in range(depth):
        j = i // 2
        if i % 2 == 0:
            h2 = _ssd_layer(h2, bsz, seq, norm_w[i], ssd_w_in[j], ssd_conv_w[j], ssd_conv_b[j],
                            ssd_dt_bias[j], ssd_a_log[j], ssd_d[j], ssd_norm_w[j], ssd_w_out[j])
        else:
            h2 = _attn_layer(h2, bsz, seq, norm_w[i], attn_w_in[j], attn_q_norm[j], attn_k_norm[j],
                             attn_lambda[j], attn_subln[j], attn_w_out[j], thr, rb_flat, ones_bd, i)
    return h2.reshape(bsz, seq, D_MODEL)
```

```python
import functools
import math

import jax
import jax.numpy as jnp
from jax import lax
from jax.experimental import pallas as pl
from jax.experimental.pallas import tpu as pltpu

F32 = jnp.float32
BF16 = jnp.bfloat16

RMS_EPS = 1e-6

D_MODEL = 1024
SSD_D_INNER = 2048
SSD_HEADDIM = 64
SSD_HEADS = 32
SSD_D_STATE = 128
SSD_GROUPS = 8
SSD_HPG = 4
SSD_CONV = 5
SSD_CHUNK = 128
SSD_GN = SSD_GROUPS * SSD_D_STATE
ATTN_HEADS = 8
ATTN_HEAD_DIM = 64
ATTN_V_HEAD = 128
ATTN_QK_DIM = 1024
ATTN_V_DIM = 1024
REL_BUCKETS = 32
REL_MAX_DIST = 128

V7X_VMEM_BYTES = 64 * 1024 * 1024
VMEM_LIMIT = V7X_VMEM_BYTES - 8 * 1024 * 1024
LANES = 128
BF16_SUBLANES = 16

ROW_TILE = 512
HALO = BF16_SUBLANES
CONV_COLS = 512
ATTN_TILE = 512
ATTN_ROWS_A = 512
ATTN_ROWS_B = 64
ATTN_ONES_ROWS = BF16_SUBLANES
LOG2E = 1.4426950408889634
GS = SSD_HPG * SSD_HEADDIM


def _resident(shape):
    nd = len(shape)
    return pl.BlockSpec(shape, lambda *_: (0,) * nd, pipeline_mode=pl.Buffered(1))


def _silu(x):
    return x * (1.0 / (1.0 + jnp.exp(-x)))


def _softplus(x):
    return jnp.maximum(x, 0.0) + jnp.log(1.0 + jnp.exp(-jnp.abs(x)))


def _rms_rows(x, w_row):
    ms = jnp.mean(x * x, axis=-1, keepdims=True)
    return x * lax.rsqrt(ms + RMS_EPS) * w_row


def _dot(a, b):
    return jnp.dot(a, b, preferred_element_type=F32)


def _dot_nt(a, b):
    return lax.dot_general(a, b, (((1,), (1,)), ((), ())), preferred_element_type=F32)


def _split3(x):
    hi = x.astype(BF16)
    r1 = x - hi.astype(F32)
    mid = r1.astype(BF16)
    lo = (r1 - mid.astype(F32)).astype(BF16)
    return hi, mid, lo


def _ssd_in_kernel(h_ref, hp_ref, hn_ref, nw_ref, wz_ref, wxbc_ref, wdc_ref, wdr_ref,
                   cw_ref, cb_ref, biasc_ref, alogc_ref, biasr_ref, alogr_ref,
                   z_ref, x_ref, b_ref, c_ref, csc_ref, dtr_ref, csr_ref, pre_ref, *, seq):
    i = pl.program_id(0)
    tm = h_ref.shape[0]
    q = SSD_CHUNK
    nw = nw_ref[...]
    at_start = (i * tm) % seq == 0
    at_end = ((i + 1) * tm) % seq == 0
    xn = _rms_rows(h_ref[...], nw).astype(BF16)
    xp = (_rms_rows(hp_ref[...], nw) * jnp.where(at_start, 0.0, 1.0)).astype(BF16)
    xq = (_rms_rows(hn_ref[...], nw) * jnp.where(at_end, 0.0, 1.0)).astype(BF16)
    ext = jnp.concatenate([xp, xn, xq], axis=0)

    z_ref[...] = _dot(xn, wz_ref[...]).astype(BF16)

    rows = lax.broadcasted_iota(jnp.int32, (q, q), 0)
    cols = lax.broadcasted_iota(jnp.int32, (q, q), 1)
    lower_b = (cols <= rows).astype(BF16)
    upper_f = (rows <= cols).astype(F32)
    lower_f = (rows >= cols).astype(F32)
    a_cols = _softplus(_dot(xn, wdc_ref[...]) + biasc_ref[...]) * -jnp.exp(alogc_ref[...])
    bwd_lane = (lax.broadcasted_iota(jnp.int32, (q, LANES), 1) & (SSD_HEADS)) != 0
    dt_rows = _softplus(_dot_nt(wdr_ref[...], xn) + biasr_ref[...])
    a_rows = dt_rows * -jnp.exp(alogr_ref[...])
    dtr_ref[...] = dt_rows
    bwd_row = (lax.broadcasted_iota(jnp.int32, (2 * SSD_HEADS, q), 0) & SSD_HPG) != 0
    for j in range(tm // q):
        blk = a_cols[j * q:(j + 1) * q, :]
        hi, mid, lo = _split3(blk)
        r3 = _dot(lower_b, jnp.concatenate([hi, mid, lo], axis=1))
        pre = r3[:, :LANES] + r3[:, LANES:2 * LANES] + r3[:, 2 * LANES:]
        suf = pre[q - 1:q, :] - pre + blk
        csc_ref[j * q:(j + 1) * q, :] = jnp.where(bwd_lane, suf, pre)
        rblk = a_rows[:, j * q:(j + 1) * q]
        pre_r = jnp.dot(rblk, upper_f, preferred_element_type=F32, precision=lax.Precision.HIGHEST)
        suf_r = jnp.dot(rblk, lower_f, preferred_element_type=F32, precision=lax.Precision.HIGHEST)
        csr_ref[:, j * q:(j + 1) * q] = jnp.where(bwd_row, suf_r, pre_r)

    half = SSD_CONV // 2
    outs = ((x_ref, 0, SSD_D_INNER), (b_ref, SSD_D_INNER, SSD_GN),
            (c_ref, SSD_D_INNER + SSD_GN, SSD_GN))
    for o_ref, base, width in outs:
        for j in range(width // CONV_COLS):
            lo = base + j * CONV_COLS
            pre_ref[...] = _dot(ext, wxbc_ref[:, lo:lo + CONV_COLS])
            w = cw_ref[:, lo:lo + CONV_COLS]
            acc = cb_ref[:, lo:lo + CONV_COLS]
            for k in range(SSD_CONV):
                off = HALO - half + k
                acc = acc + pre_ref[off:off + tm, :] * w[k:k + 1, :]
            o_ref[:, j * CONV_COLS:(j + 1) * CONV_COLS] = _silu(acc).astype(BF16)


def _ssd_in_proj(h2, seq, nw, wz, wxbc, wdc, wdr, conv_w, conv_b, biasc, alogc, biasr, alogr):
    t = h2.shape[0]
    tm = min(ROW_TILE, seq)
    assert seq % tm == 0 and tm % HALO == 0 and tm % SSD_CHUNK == 0 and biasr.shape[1] == tm
    per = tm // HALO
    last = t // HALO - 1
    row = lambda n: pl.BlockSpec((tm, n), lambda i: (i, 0))
    return pl.pallas_call(
        functools.partial(_ssd_in_kernel, seq=seq),
        name="ssd_in_proj",
        grid=(t // tm,),
        in_specs=[row(D_MODEL),
                  pl.BlockSpec((HALO, D_MODEL), lambda i: (jnp.maximum(i * per - 1, 0), 0)),
                  pl.BlockSpec((HALO, D_MODEL), lambda i: (jnp.minimum((i + 1) * per, last), 0)),
                  _resident(nw.shape), _resident(wz.shape), _resident(wxbc.shape),
                  _resident(wdc.shape), _resident(wdr.shape), _resident(conv_w.shape),
                  _resident(conv_b.shape), _resident(biasc.shape), _resident(alogc.shape),
                  _resident(biasr.shape), _resident(alogr.shape)],
        out_specs=[row(SSD_D_INNER), row(SSD_D_INNER), row(SSD_GN), row(SSD_GN), row(LANES),
                   pl.BlockSpec((2 * SSD_HEADS, tm), lambda i: (0, i)),
                   pl.BlockSpec((2 * SSD_HEADS, tm), lambda i: (0, i))],
        out_shape=[jax.ShapeDtypeStruct((t, SSD_D_INNER), BF16),
                   jax.ShapeDtypeStruct((t, SSD_D_INNER), BF16),
                   jax.ShapeDtypeStruct((t, SSD_GN), BF16),
                   jax.ShapeDtypeStruct((t, SSD_GN), BF16),
                   jax.ShapeDtypeStruct((t, LANES), F32),
                   jax.ShapeDtypeStruct((2 * SSD_HEADS, t), F32),
                   jax.ShapeDtypeStruct((2 * SSD_HEADS, t), F32)],
        scratch_shapes=[pltpu.VMEM((tm + 2 * HALO, CONV_COLS), F32)],
        compiler_params=pltpu.CompilerParams(dimension_semantics=("arbitrary",),
                                             vmem_limit_bytes=VMEM_LIMIT),
    )(h2, h2, h2, nw, wz, wxbc, wdc, wdr, conv_w, conv_b, biasc, alogc, biasr, alogr)


def _ssd_scan_kernel(x_ref, b_ref, c_ref, csc_ref, dtr_ref, csr_ref, d_ref,
                     y_ref, acc_ref, stf_ref, stb_ref, e_ref, *, seq):
    g = pl.program_id(1)
    q = SSD_CHUNK
    nchunk = seq // q

    rows = lax.broadcasted_iota(jnp.int32, (q, q), 0)
    cols = lax.broadcasted_iota(jnp.int32, (q, q), 1)
    lane = lax.broadcasted_iota(jnp.int32, (q, LANES), 1)
    lane_gs = lax.broadcasted_iota(jnp.int32, (q, GS), 1)

    ek = lax.broadcasted_iota(jnp.int32, (2 * LANES, SSD_HPG * q), 0)
    en = lax.broadcasted_iota(jnp.int32, (2 * LANES, SSD_HPG * q), 1) >> 7
    for d in range(2):
        col = d * SSD_HEADS + g * SSD_HPG + en
        e_ref[d] = ((ek == col) | (ek == col + 64) | (ek == col + LANES)).astype(BF16)

    d_row = d_ref[...]

    def front(c, rev):
        s0 = pl.multiple_of(c * q, q)
        xs_b = x_ref[0, pl.ds(s0, q), :]
        bm_b = b_ref[0, pl.ds(s0, q), :]
        cm_b = c_ref[0, pl.ds(s0, q), :]
        gmat = _dot_nt(cm_b, bm_b)
        bm_t = bm_b.astype(F32).T
        keep = (cols >= rows) if rev else (cols <= rows)
        grp_rows = pl.ds(pl.multiple_of(g * 8, 8), 8)
        return dict(xs_b=xs_b, cm_b=cm_b, gmat=gmat, bm_t=bm_t, keep=keep,
                    dt_rows=dtr_ref[grp_rows, pl.ds(s0, q)],
                    cs_rows=csr_ref[grp_rows, pl.ds(s0, q)],
                    cs_cols=csc_ref[0, pl.ds(s0, q), :])

    def bcast_lhs(f):
        hi, mid, lo = _split3(f["cs_cols"])
        return jnp.concatenate([jnp.where(lane < 64, hi, mid), lo], axis=1)

    def middle(f, bc, rev):
        d = 1 if rev else 0
        csx = jnp.concatenate(
            [jnp.where(lane < 64, bc[:, 0:q], bc[:, q:2 * q]),
             jnp.where(lane < 64, bc[:, 2 * q:3 * q], bc[:, 3 * q:4 * q])], axis=1)
        mmats, bts, xms = [], [], []
        for r in range(SSD_HPG):
            row = d * SSD_HPG + r
            cs_r = f["cs_rows"][row:row + 1, :]
            dt_r = f["dt_rows"][row:row + 1, :]
            end_r = cs_r[:, 0:1] if rev else cs_r[:, q - 1:q]
            seg = bc[:, r * q:(r + 1) * q] - cs_r
            lmat = jnp.exp(jnp.minimum(seg, 0.0))
            mmats.append(jnp.where(f["keep"], f["gmat"] * lmat * dt_r, 0.0).astype(BF16))
            xms.append(jnp.where((lane_gs >> 6) == r, f["xs_b"], jnp.zeros_like(f["xs_b"])))
            w_r = jnp.exp(end_r - cs_r) * dt_r
            bts.append((f["bm_t"] * w_r).astype(BF16))
        pairs = [(jnp.concatenate([jnp.concatenate(mmats[r:r + 2], axis=1),
                                   jnp.concatenate(bts[r:r + 2], axis=1)], axis=0),
                  jnp.concatenate(xms[r:r + 2], axis=0)) for r in range(0, SSD_HPG, 2)]
        return dict(cm_b=f["cm_b"], xs_b=f["xs_b"], decay=jnp.exp(csx),
                    end_decay=jnp.exp(csx[0:1, :] if rev else csx[q - 1:q, :]), pairs=pairs)

    def back(m, rev):
        st_ref = stb_ref if rev else stf_ref
        state = st_ref[...]
        y = _dot(m["cm_b"], state.astype(BF16)) * m["decay"]
        dstate = jnp.zeros((SSD_D_STATE, GS), F32)
        for lhs, xpair in m["pairs"]:
            both = _dot(lhs, xpair)
            y = y + both[0:q, :]
            dstate = dstate + both[q:, :]
        st_ref[...] = state * m["end_decay"] + dstate
        return y

    def step(i, finish):
        work = ((2 * i, False), (2 * i + 1, False),
                (nchunk - 1 - 2 * i, True), (nchunk - 2 - 2 * i, True))
        fronts = [front(c, rev) for c, rev in work]
        bcs = []
        for d in range(2):
            both = _dot(jnp.concatenate([bcast_lhs(fronts[2 * d]), bcast_lhs(fronts[2 * d + 1])],
                                        axis=0), e_ref[d])
            bcs += [both[0:q, :], both[q:, :]]
        mids = [middle(f, bc, rev) for f, bc, (_, rev) in zip(fronts, bcs, work)]
        for k in (0, 2, 1, 3):
            m, (c, rev) = mids[k], work[k]
            y = back(m, rev)
            if not rev:
                y = y + m["xs_b"].astype(F32) * d_row
            rs = pl.ds(pl.multiple_of(c * q, q), q)
            if finish:
                y_ref[0, rs, :] = (acc_ref[rs, :] + y).astype(y_ref.dtype)
            else:
                acc_ref[rs, :] = y

    stf_ref[...] = jnp.zeros_like(stf_ref)
    stb_ref[...] = jnp.zeros_like(stb_ref)

    def first_half(i, carry):
        step(i, False)
        return carry

    def second_half(i, carry):
        step(i, True)
        return carry

    lax.fori_loop(0, nchunk // 4, first_half, 0)
    lax.fori_loop(nchunk // 4, nchunk // 2, second_half, 0)


def _ssd_scan(x3, b3, c3, csc3, dtr, csr, d_exp):
    bsz, seq, _ = x3.shape
    assert seq % (4 * SSD_CHUNK) == 0
    grp = lambda n: pl.BlockSpec((1, seq, n), lambda b, g: (b, 0, g))
    heads_rows = pl.BlockSpec((2 * SSD_HEADS, seq), lambda b, g: (0, b))
    return pl.pallas_call(
        functools.partial(_ssd_scan_kernel, seq=seq),
        name="ssd_scan",
        grid=(bsz, SSD_GROUPS),
        in_specs=[grp(GS), grp(SSD_D_STATE), grp(SSD_D_STATE),
                  pl.BlockSpec((1, seq, LANES), lambda b, g: (b, 0, 0)),
                  heads_rows, heads_rows,
                  pl.BlockSpec((1, GS), lambda b, g: (0, g))],
        out_specs=pl.BlockSpec((1, seq, GS), lambda b, g: (b, 0, g)),
        out_shape=jax.ShapeDtypeStruct((bsz, seq, SSD_D_INNER), BF16),
        scratch_shapes=[pltpu.VMEM((seq, GS), F32),
                        pltpu.VMEM((SSD_D_STATE, GS), F32),
                        pltpu.VMEM((SSD_D_STATE, GS), F32),
                        pltpu.VMEM((2, 2 * LANES, SSD_HPG * SSD_CHUNK), BF16)],
        compiler_params=pltpu.CompilerParams(dimension_semantics=("arbitrary", "arbitrary"),
                                             vmem_limit_bytes=VMEM_LIMIT),
    )(x3, b3, c3, csc3, dtr, csr, d_exp)


def _ssd_out_kernel(y_ref, z_ref, nw_ref, w_ref, h_ref, o_ref):
    u = y_ref[...].astype(F32) * _silu(z_ref[...].astype(F32))
    un = _rms_rows(u, nw_ref[...]).astype(BF16)
    o_ref[...] = h_ref[...] + _dot(un, w_ref[...])


def _ssd_out_proj(y2, z2, nw, w, h2):
    t = h2.shape[0]
    tm = min(ROW_TILE, t)
    row = lambda n: pl.BlockSpec((tm, n), lambda i: (i, 0))
    return pl.pallas_call(
        _ssd_out_kernel,
        name="ssd_out_proj",
        grid=(t // tm,),
        in_specs=[row(SSD_D_INNER), row(SSD_D_INNER), _resident(nw.shape), _resident(w.shape),
                  row(D_MODEL)],
        out_specs=row(D_MODEL),
        out_shape=jax.ShapeDtypeStruct((t, D_MODEL), F32),
        compiler_params=pltpu.CompilerParams(dimension_semantics=("arbitrary",),
                                             vmem_limit_bytes=VMEM_LIMIT),
    )(y2, z2, nw, w, h2)


def _attn_in_kernel(h_ref, nw_ref, wqt_ref, wk_ref, wvt_ref, wz_ref, qn_ref, kn_ref, ones_ref,
                    qt_ref, k_ref, vt_ref, z_ref):
    xn = _rms_rows(h_ref[...], nw_ref[...]).astype(BF16)
    inv_d = 1.0 / ATTN_HEAD_DIM

    tm = h_ref.shape[0]
    yq = _dot_nt(wqt_ref[...], xn).reshape(-1, ATTN_HEAD_DIM, tm)
    ssq = jnp.sum(yq * yq, axis=1, keepdims=True)
    qn = (yq * lax.rsqrt(ssq * inv_d + RMS_EPS)).reshape(ATTN_QK_DIM, tm)
    qt_ref[...] = (qn * qn_ref[...]).astype(BF16)

    yk = _dot(xn, wk_ref[...])
    ssk = _dot((yk * yk).astype(BF16), ones_ref[...])
    k_ref[...] = (yk * lax.rsqrt(ssk * inv_d + RMS_EPS) * kn_ref[...]).astype(BF16)

    vt_ref[...] = _dot_nt(wvt_ref[...], xn).astype(BF16)
    z_ref[...] = _dot(xn, wz_ref[...]).astype(BF16)


def _attn_in_proj(h2, nw, wqt, wk, wvt, wz, qn_cols, kn, ones_bd):
    t = h2.shape[0]
    tm = qn_cols.shape[1]
    row = lambda n: pl.BlockSpec((tm, n), lambda i: (i, 0))
    col = lambda n: pl.BlockSpec((n, tm), lambda i: (0, i))
    return pl.pallas_call(
        _attn_in_kernel,
        name="attn_in_proj",
        grid=(t // tm,),
        in_specs=[row(D_MODEL)] + [_resident(a.shape)
                                   for a in (nw, wqt, wk, wvt, wz, qn_cols, kn, ones_bd)],
        out_specs=[col(ATTN_QK_DIM), row(ATTN_QK_DIM), col(ATTN_V_DIM), row(ATTN_V_DIM)],
        out_shape=[jax.ShapeDtypeStruct((ATTN_QK_DIM, t), BF16),
                   jax.ShapeDtypeStruct((t, ATTN_QK_DIM), BF16),
                   jax.ShapeDtypeStruct((ATTN_V_DIM, t), BF16),
                   jax.ShapeDtypeStruct((t, ATTN_V_DIM), BF16)],
        compiler_params=pltpu.CompilerParams(dimension_semantics=("arbitrary",),
                                             vmem_limit_bytes=VMEM_LIMIT),
    )(h2, nw, wqt, wk, wvt, wz, qn_cols, kn, ones_bd)


def _attn_kernel(thr_ref, rb_ref, qt_ref, k_ref, vt_ref, z_ref, lam_ref, subln_ref, o_ref,
                 bias_ref, sa_ref, sb_ref, p_ref, m_ref, acc_ref, vte_ref, *, seq, tile, lam_init):
    h = pl.program_id(0)
    b = pl.program_id(1)
    qi = pl.program_id(2)
    nk = seq // tile
    half_buckets = REL_BUCKETS // 2
    sub = 8

    @pl.when((b == 0) & (qi == 0))
    def _build_bias():
        rel0 = (lax.broadcasted_iota(jnp.int32, (tile, tile), 0)
                - lax.broadcasted_iota(jnp.int32, (tile, tile), 1))
        for idx in range(3):
            rel = rel0 + (idx - 1) * tile
            n = jnp.abs(rel)
            neg = jnp.full((tile, tile), rb_ref[h], F32)
            pos = jnp.full((tile, tile), rb_ref[half_buckets * ATTN_HEADS + h], F32)
            for t in range(1, half_buckets):
                ge = n >= thr_ref[t]
                neg = jnp.where(ge, rb_ref[t * ATTN_HEADS + h], neg)
                pos = jnp.where(ge, rb_ref[(half_buckets + t) * ATTN_HEADS + h], pos)
            bias_ref[idx] = jnp.where(rel > 0, pos, neg) * LOG2E

    @pl.when(qi == 0)
    def _extend_v():
        vte_ref[0:ATTN_V_HEAD, :] = vt_ref[...]
        vte_ref[ATTN_V_HEAD:, :] = jnp.ones((ATTN_ONES_ROWS, seq), BF16)

    c_left = rb_ref[(half_buckets - 1) * ATTN_HEADS + h] * LOG2E
    c_right = rb_ref[(2 * half_buckets - 1) * ATTN_HEADS + h] * LOG2E

    def is_near(kc):
        return (kc - qi >= -1) & (kc - qi <= 1)

    def far_bias(kc):
        return jnp.where(is_near(kc), 0.0, jnp.where(kc < qi, c_left, c_right))

    q_t = qt_ref[...]
    row = lax.broadcasted_iota(jnp.int32, (ATTN_V_HEAD, tile), 0)
    zero = jnp.zeros_like(q_t)
    qs = (jnp.where(row < ATTN_HEAD_DIM, q_t, zero), jnp.where(row >= ATTN_HEAD_DIM, q_t, zero))

    m_ref[...] = jnp.full(m_ref.shape, -jnp.inf, F32)
    acc_ref[...] = jnp.zeros(acc_ref.shape, F32)

    def score_block(kc, t, r, s_out, m8, with_bias):
        rows = pl.ds(r * ATTN_ROWS_A, ATTN_ROWS_A)
        kt = k_ref[0, pl.ds(pl.multiple_of(kc * tile, tile) + r * ATTN_ROWS_A, ATTN_ROWS_A), :]
        sb = _dot(kt, qs[t])
        if with_bias:
            sb = sb + bias_ref[jnp.clip(kc - qi + 1, 0, 2), rows, :]
        s_out[t, rows, :] = sb
        for i in range(ATTN_ROWS_A // sub):
            m8 = jnp.maximum(m8, sb[i * sub:(i + 1) * sub, :])
        return m8

    def step_body(kc_next, s_next, kc, s_cur, m8s, with_bias):
        vt = vte_ref[:, pl.ds(pl.multiple_of(kc * tile, tile), tile)]
        shift = far_bias(kc)
        nblk = tile // ATTN_ROWS_A
        per = ATTN_ROWS_A // ATTN_ROWS_B
        m8s_next = []
        for t in range(2):
            m_old = m_ref[t]
            m_new = jnp.maximum(m_old, jnp.max(m8s[t], axis=0, keepdims=True) + shift)
            alpha = jnp.exp2(m_old - m_new)
            msub = m_new - shift
            m8 = jnp.full((sub, tile), -jnp.inf, F32)
            for r in range(nblk):
                m8 = score_block(kc_next, t, r, s_next, m8, with_bias)
                for rb in range(r * per, (r + 1) * per):
                    rows = pl.ds(rb * ATTN_ROWS_B, ATTN_ROWS_B)
                    p_ref[t, rows, :] = jnp.exp2(s_cur[t, rows, :] - msub).astype(BF16)
            acc_ref[t] = alpha * acc_ref[t] + _dot(vt, p_ref[t])
            m_ref[t] = m_new
            m8s_next.append(m8)
        return tuple(m8s_next)

    def step(kc_next, s_next, kc, s_cur, m8s):
        return lax.cond(is_near(kc_next),
                        lambda: step_body(kc_next, s_next, kc, s_cur, m8s, True),
                        lambda: step_body(kc_next, s_next, kc, s_cur, m8s, False))

    def chunk_pair(j, m8s):
        c0 = 2 * j
        m8s_odd = step(c0 + 1, sb_ref, c0, sa_ref, m8s)
        return step(jnp.minimum(c0 + 2, nk - 1), sa_ref, c0 + 1, sb_ref, m8s_odd)

    def first_scores(with_bias):
        def fn():
            m8s0 = []
            for t in range(2):
                m8 = jnp.full((sub, tile), -jnp.inf, F32)
                for r in range(tile // ATTN_ROWS_A):
                    m8 = score_block(0, t, r, sa_ref, m8, with_bias)
                m8s0.append(m8)
            return tuple(m8s0)
        return fn

    lax.fori_loop(0, nk // 2, chunk_pair,
                  lax.cond(is_near(0), first_scores(True), first_scores(False)))

    lv = lam_ref[...]
    lam = (jnp.exp(jnp.sum(lv[0:1] * lv[1:2], axis=-1, keepdims=True))
           - jnp.exp(jnp.sum(lv[2:3] * lv[3:4], axis=-1, keepdims=True)) + lam_init)
    o1 = acc_ref[0, 0:ATTN_V_HEAD, :] / acc_ref[0, ATTN_V_HEAD:ATTN_V_HEAD + 1, :]
    o2 = acc_ref[1, 0:ATTN_V_HEAD, :] / acc_ref[1, ATTN_V_HEAD:ATTN_V_HEAD + 1, :]
    o = (o1 - lam * o2).T
    o = _rms_rows(o, subln_ref[...]) * (1.0 - lam_init)
    o_ref[0] = (o * _silu(z_ref[0].astype(F32))).astype(o_ref.dtype)


def _attention(thr, rb_flat, qt, k3, vt, z3, lam_vec, subln, lam_init):
    bsz, seq, _ = k3.shape
    tile = min(ATTN_TILE, seq)
    nq = seq // tile
    assert tile > REL_MAX_DIST and seq % (2 * tile) == 0
    smem = pl.BlockSpec(memory_space=pltpu.SMEM)
    qspec = pl.BlockSpec((ATTN_V_HEAD, tile), lambda h, b, i: (h, b * nq + i))
    kspec = pl.BlockSpec((1, seq, ATTN_V_HEAD), lambda h, b, i: (b, 0, h))
    vspec = pl.BlockSpec((ATTN_V_HEAD, seq), lambda h, b, i: (h, b))
    zspec = pl.BlockSpec((1, tile, ATTN_V_HEAD), lambda h, b, i: (b, i, h))
    return pl.pallas_call(
        functools.partial(_attn_kernel, seq=seq, tile=tile, lam_init=lam_init),
        name="diff_attention",
        grid=(ATTN_HEADS, bsz, nq),
        in_specs=[smem, smem, qspec, kspec, vspec, zspec,
                  pl.BlockSpec(lam_vec.shape, lambda h, b, i: (0, 0)),
                  pl.BlockSpec(subln.shape, lambda h, b, i: (0, 0))],
        out_specs=zspec,
        out_shape=jax.ShapeDtypeStruct((bsz, seq, ATTN_V_DIM), BF16),
        scratch_shapes=[pltpu.VMEM((3, tile, tile), F32),
                        pltpu.VMEM((2, tile, tile), F32),
                        pltpu.VMEM((2, tile, tile), F32),
                        pltpu.VMEM((2, tile, tile), BF16),
                        pltpu.VMEM((2, 1, tile), F32),
                        pltpu.VMEM((2, ATTN_V_HEAD + ATTN_ONES_ROWS, tile), F32),
                        pltpu.VMEM((ATTN_V_HEAD + ATTN_ONES_ROWS, seq), BF16)],
        compiler_params=pltpu.CompilerParams(
            dimension_semantics=("arbitrary", "arbitrary", "arbitrary"),
            vmem_limit_bytes=VMEM_LIMIT),
    )(thr, rb_flat, qt, k3, vt, z3, lam_vec, subln)


def _attn_out_kernel(o_ref, w_ref, h_ref, out_ref):
    out_ref[...] = h_ref[...] + _dot(o_ref[...], w_ref[...])


def _attn_out_proj(o2, w, h2):
    t = h2.shape[0]
    tm = min(ROW_TILE, t)
    row = lambda n: pl.BlockSpec((tm, n), lambda i: (i, 0))
    return pl.pallas_call(
        _attn_out_kernel,
        name="attn_out_proj",
        grid=(t // tm,),
        in_specs=[row(ATTN_V_DIM), _resident(w.shape), row(D_MODEL)],
        out_specs=row(D_MODEL),
        out_shape=jax.ShapeDtypeStruct((t, D_MODEL), F32),
        compiler_params=pltpu.CompilerParams(dimension_semantics=("arbitrary",),
                                             vmem_limit_bytes=VMEM_LIMIT),
    )(o2, w, h2)


def _ssd_layer(h2, bsz, seq, nw, w_in, conv_w, conv_b, dt_bias, a_log, d_skip, norm_w, w_out):
    wb16 = w_in.astype(BF16)
    o_x = SSD_D_INNER
    o_b = 2 * SSD_D_INNER
    o_c = o_b + SSD_GN
    o_dt = o_c + SSD_GN
    wdt = wb16[:, o_dt:]
    wdc = jnp.concatenate([wdt, wdt], axis=1)
    perm = lambda a: a.reshape(2, SSD_GROUPS, SSD_HPG).transpose(1, 0, 2).reshape(2 * SSD_HEADS)
    wdr = wdt.T.reshape(2, SSD_GROUPS, SSD_HPG, D_MODEL).transpose(1, 0, 2, 3).reshape(
        2 * SSD_HEADS, D_MODEL)
    del o_b, o_c
    tm = min(ROW_TILE, seq)
    flat = lambda a: a.reshape(2 * SSD_HEADS).astype(F32)
    biasc = jnp.tile(flat(dt_bias), 2).reshape(1, LANES)
    alogc = jnp.tile(flat(a_log), 2).reshape(1, LANES)
    biasr = jnp.broadcast_to(perm(flat(dt_bias))[:, None], (2 * SSD_HEADS, tm))
    alogr = jnp.broadcast_to(perm(flat(a_log))[:, None], (2 * SSD_HEADS, tm))
    d_exp = jnp.repeat(d_skip.astype(F32), SSD_HEADDIM).reshape(1, SSD_D_INNER)
    z2, x2, b2, c2, csc, dtr, csr = _ssd_in_proj(
        h2, seq, nw.reshape(1, -1), wb16[:, :o_x], wb16[:, o_x:o_dt], wdc, wdr,
        conv_w.astype(F32), conv_b.reshape(1, -1).astype(F32), biasc, alogc, biasr, alogr)

    y3 = _ssd_scan(x2.reshape(bsz, seq, -1), b2.reshape(bsz, seq, -1), c2.reshape(bsz, seq, -1),
                   csc.reshape(bsz, seq, LANES), dtr, csr, d_exp)
    return _ssd_out_proj(y3.reshape(bsz * seq, -1), z2, norm_w.reshape(1, -1),
                         w_out.astype(BF16), h2)


def _t5_thresholds():
    nb = REL_BUCKETS // 2
    max_exact = nb // 2
    n = jnp.arange(REL_MAX_DIST + 1, dtype=jnp.int32)
    nf = jnp.maximum(n, 1).astype(F32)
    large = max_exact + (jnp.log(nf / max_exact) / math.log(REL_MAX_DIST / max_exact)
                         * (nb - max_exact)).astype(jnp.int32)
    bucket = jnp.where(n < max_exact, n, jnp.minimum(large, nb - 1))
    t = jnp.arange(nb, dtype=jnp.int32)
    return jnp.sum((bucket[None, :] < t[:, None]).astype(jnp.int32), axis=1)


def _attn_layer(h2, bsz, seq, nw, w_in, q_norm, k_norm, lam_vec, subln, w_out, thr, rb_flat,
                ones_bd, layer_idx):
    wb16 = w_in.astype(BF16)
    reps = ATTN_QK_DIM // ATTN_HEAD_DIM
    tm = min(ROW_TILE, bsz * seq)
    qn = jnp.tile(q_norm.astype(F32), reps) * (ATTN_HEAD_DIM ** -0.5 * LOG2E)
    qn_cols = jnp.broadcast_to(qn[:, None], (ATTN_QK_DIM, tm))
    kn = jnp.tile(k_norm.astype(F32), reps).reshape(1, -1)
    qt, k2, vt, z2 = _attn_in_proj(
        h2, nw.reshape(1, -1), wb16[:, :ATTN_QK_DIM].T, wb16[:, ATTN_QK_DIM:2 * ATTN_QK_DIM],
        wb16[:, 2 * ATTN_QK_DIM:2 * ATTN_QK_DIM + ATTN_V_DIM].T,
        wb16[:, 2 * ATTN_QK_DIM + ATTN_V_DIM:], qn_cols, kn, ones_bd)
    lam_init = 0.8 - 0.6 * math.exp(-0.3 * layer_idx)
    r3 = lambda a: a.reshape(bsz, seq, -1)
    o3 = _attention(thr, rb_flat, qt, r3(k2), vt, r3(z2), lam_vec.astype(F32),
                    subln.reshape(1, -1).astype(F32), lam_init)
    return _attn_out_proj(o3.reshape(bsz * seq, -1), w_out.astype(BF16), h2)


def kernel(x, norm_w, ssd_w_in, ssd_conv_w, ssd_conv_b, ssd_dt_bias, ssd_a_log, ssd_d, ssd_norm_w, ssd_w_out, attn_w_in, attn_q_norm, attn_k_norm, attn_lambda, attn_subln, attn_w_out, rel_bias):
    bsz, seq, _ = x.shape
    depth = norm_w.shape[0]
    h2 = x.reshape(bsz * seq, D_MODEL)
    thr = _t5_thresholds()
    rb_flat = rel_bias.astype(F32).reshape(-1)
    blk = jnp.arange(ATTN_QK_DIM, dtype=jnp.int32) // ATTN_HEAD_DIM
    ones_bd = (blk[:, None] == blk[None, :]).astype(BF16)
    for i in range(depth):
        j = i // 2
        if i % 2 == 0:
            h2 = _ssd_layer(h2, bsz, seq, norm_w[i], ssd_w_in[j], ssd_conv_w[j], ssd_conv_b[j],
                            ssd_dt_bias[j], ssd_a_log[j], ssd_d[j], ssd_norm_w[j], ssd_w_out[j])
        else:
            h2 = _attn_layer(h2, bsz, seq, norm_w[i], attn_w_in[j], attn_q_norm[j], attn_k_norm[j],
                             attn_lambda[j], attn_subln[j], attn_w_out[j], thr, rb_flat, ones_bd, i)
    return h2.reshape(bsz, seq, D_MODEL)
```

```python
import functools
import math

import jax
import jax.numpy as jnp
from jax import lax
from jax.experimental import pallas as pl
from jax.experimental.pallas import tpu as pltpu

F32 = jnp.float32
BF16 = jnp.bfloat16

RMS_EPS = 1e-6

D_MODEL = 1024
SSD_D_INNER = 2048
SSD_HEADDIM = 64
SSD_HEADS = 32
SSD_D_STATE = 128
SSD_GROUPS = 8
SSD_HPG = 4
SSD_CONV = 5
SSD_CHUNK = 128
SSD_GN = SSD_GROUPS * SSD_D_STATE
ATTN_HEADS = 8
ATTN_HEAD_DIM = 64
ATTN_V_HEAD = 128
ATTN_QK_DIM = 1024
ATTN_V_DIM = 1024
REL_BUCKETS = 32
REL_MAX_DIST = 128

V7X_VMEM_BYTES = 64 * 1024 * 1024
VMEM_LIMIT = V7X_VMEM_BYTES - 8 * 1024 * 1024
LANES = 128
BF16_SUBLANES = 16

ROW_TILE = 512
HALO = BF16_SUBLANES
CONV_COLS = 512
ATTN_TILE = 512
ATTN_KTILE = 1024
ATTN_ROWS_B = 64
ATTN_ONES_ROWS = BF16_SUBLANES
LOG2E = 1.4426950408889634
GS = SSD_HPG * SSD_HEADDIM
SCAN_DEPTH = 2


def _resident(shape):
    nd = len(shape)
    return pl.BlockSpec(shape, lambda *_: (0,) * nd, pipeline_mode=pl.Buffered(1))


def _silu(x):
    return x * (1.0 / (1.0 + jnp.exp(-x)))


def _softplus(x):
    return jnp.maximum(x, 0.0) + jnp.log(1.0 + jnp.exp(-jnp.abs(x)))


def _rms_rows(x, w_row):
    ms = jnp.mean(x * x, axis=-1, keepdims=True)
    return x * lax.rsqrt(ms + RMS_EPS) * w_row


def _dot(a, b):
    return jnp.dot(a, b, preferred_element_type=F32)


def _dot_nt(a, b):
    return lax.dot_general(a, b, (((1,), (1,)), ((), ())), preferred_element_type=F32)


def _split3(x):
    hi = x.astype(BF16)
    r1 = x - hi.astype(F32)
    mid = r1.astype(BF16)
    lo = (r1 - mid.astype(F32)).astype(BF16)
    return hi, mid, lo


def _ssd_in_kernel(h_ref, hp_ref, hn_ref, nw_ref, wz_ref, wxbc_ref, wdc_ref, wdr_ref,
                   cw_ref, cb_ref, biasc_ref, alogc_ref, biasr_ref, alogr_ref,
                   z_ref, x_ref, b_ref, c_ref, csc_ref, dtr_ref, csr_ref, pre_ref, *, seq):
    i = pl.program_id(0)
    tm = h_ref.shape[0]
    q = SSD_CHUNK
    nw = nw_ref[...]
    at_start = (i * tm) % seq == 0
    at_end = ((i + 1) * tm) % seq == 0
    xn = _rms_rows(h_ref[...], nw).astype(BF16)
    xp = (_rms_rows(hp_ref[...], nw) * jnp.where(at_start, 0.0, 1.0)).astype(BF16)
    xq = (_rms_rows(hn_ref[...], nw) * jnp.where(at_end, 0.0, 1.0)).astype(BF16)
    ext = jnp.concatenate([xp, xn, xq], axis=0)

    z_ref[...] = _dot(xn, wz_ref[...]).astype(BF16)

    rows = lax.broadcasted_iota(jnp.int32, (q, q), 0)
    cols = lax.broadcasted_iota(jnp.int32, (q, q), 1)
    lower_b = (cols <= rows).astype(BF16)
    upper_f = (rows <= cols).astype(F32)
    lower_f = (rows >= cols).astype(F32)
    a_cols = _softplus(_dot(xn, wdc_ref[...]) + biasc_ref[...]) * -jnp.exp(alogc_ref[...])
    bwd_lane = (lax.broadcasted_iota(jnp.int32, (q, LANES), 1) & (SSD_HEADS)) != 0
    dt_rows = _softplus(_dot_nt(wdr_ref[...], xn) + biasr_ref[...])
    a_rows = dt_rows * -jnp.exp(alogr_ref[...])
    dtr_ref[...] = dt_rows
    bwd_row = (lax.broadcasted_iota(jnp.int32, (2 * SSD_HEADS, q), 0) & SSD_HPG) != 0
    for j in range(tm // q):
        blk = a_cols[j * q:(j + 1) * q, :]
        hi, mid, lo = _split3(blk)
        r3 = _dot(lower_b, jnp.concatenate([hi, mid, lo], axis=1))
        pre = r3[:, :LANES] + r3[:, LANES:2 * LANES] + r3[:, 2 * LANES:]
        suf = pre[q - 1:q, :] - pre + blk
        csc_ref[j * q:(j + 1) * q, :] = jnp.where(bwd_lane, suf, pre)
        rblk = a_rows[:, j * q:(j + 1) * q]
        pre_r = jnp.dot(rblk, upper_f, preferred_element_type=F32, precision=lax.Precision.HIGHEST)
        suf_r = jnp.dot(rblk, lower_f, preferred_element_type=F32, precision=lax.Precision.HIGHEST)
        csr_ref[:, j * q:(j + 1) * q] = jnp.where(bwd_row, suf_r, pre_r)

    half = SSD_CONV // 2
    outs = ((x_ref, 0, SSD_D_INNER), (b_ref, SSD_D_INNER, SSD_GN),
            (c_ref, SSD_D_INNER + SSD_GN, SSD_GN))
    for o_ref, base, width in outs:
        for j in range(width // CONV_COLS):
            lo = base + j * CONV_COLS
            pre_ref[...] = _dot(ext, wxbc_ref[:, lo:lo + CONV_COLS])
            w = cw_ref[:, lo:lo + CONV_COLS]
            acc = cb_ref[:, lo:lo + CONV_COLS]
            for k in range(SSD_CONV):
                off = HALO - half + k
                acc = acc + pre_ref[off:off + tm, :] * w[k:k + 1, :]
            o_ref[:, j * CONV_COLS:(j + 1) * CONV_COLS] = _silu(acc).astype(BF16)


def _ssd_in_proj(h2, seq, nw, wz, wxbc, wdc, wdr, conv_w, conv_b, biasc, alogc, biasr, alogr):
    t = h2.shape[0]
    tm = min(ROW_TILE, seq)
    assert seq % tm == 0 and tm % HALO == 0 and tm % SSD_CHUNK == 0 and biasr.shape[1] == tm
    per = tm // HALO
    last = t // HALO - 1
    row = lambda n: pl.BlockSpec((tm, n), lambda i: (i, 0))
    return pl.pallas_call(
        functools.partial(_ssd_in_kernel, seq=seq),
        name="ssd_in_proj",
        grid=(t // tm,),
        in_specs=[row(D_MODEL),
                  pl.BlockSpec((HALO, D_MODEL), lambda i: (jnp.maximum(i * per - 1, 0), 0)),
                  pl.BlockSpec((HALO, D_MODEL), lambda i: (jnp.minimum((i + 1) * per, last), 0)),
                  _resident(nw.shape), _resident(wz.shape), _resident(wxbc.shape),
                  _resident(wdc.shape), _resident(wdr.shape), _resident(conv_w.shape),
                  _resident(conv_b.shape), _resident(biasc.shape), _resident(alogc.shape),
                  _resident(biasr.shape), _resident(alogr.shape)],
        out_specs=[row(SSD_D_INNER), row(SSD_D_INNER), row(SSD_GN), row(SSD_GN), row(LANES),
                   pl.BlockSpec((2 * SSD_HEADS, tm), lambda i: (0, i)),
                   pl.BlockSpec((2 * SSD_HEADS, tm), lambda i: (0, i))],
        out_shape=[jax.ShapeDtypeStruct((t, SSD_D_INNER), BF16),
                   jax.ShapeDtypeStruct((t, SSD_D_INNER), BF16),
                   jax.ShapeDtypeStruct((t, SSD_GN), BF16),
                   jax.ShapeDtypeStruct((t, SSD_GN), BF16),
                   jax.ShapeDtypeStruct((t, LANES), F32),
                   jax.ShapeDtypeStruct((2 * SSD_HEADS, t), F32),
                   jax.ShapeDtypeStruct((2 * SSD_HEADS, t), F32)],
        scratch_shapes=[pltpu.VMEM((tm + 2 * HALO, CONV_COLS), F32)],
        compiler_params=pltpu.CompilerParams(dimension_semantics=("arbitrary",),
                                             vmem_limit_bytes=VMEM_LIMIT),
    )(h2, h2, h2, nw, wz, wxbc, wdc, wdr, conv_w, conv_b, biasc, alogc, biasr, alogr)


def _ssd_scan_kernel(x_ref, b_ref, c_ref, csc_ref, dtr_ref, csr_ref, d_ref,
                     y_ref, acc_ref, stf_ref, stb_ref, e_ref, *, seq):
    g = pl.program_id(1)
    q = SSD_CHUNK
    nchunk = seq // q

    rows = lax.broadcasted_iota(jnp.int32, (q, q), 0)
    cols = lax.broadcasted_iota(jnp.int32, (q, q), 1)
    lane = lax.broadcasted_iota(jnp.int32, (q, LANES), 1)
    lane_gs = lax.broadcasted_iota(jnp.int32, (q, GS), 1)

    ek = lax.broadcasted_iota(jnp.int32, (2 * LANES, SSD_HPG * q), 0)
    en = lax.broadcasted_iota(jnp.int32, (2 * LANES, SSD_HPG * q), 1) >> 7
    for d in range(2):
        col = d * SSD_HEADS + g * SSD_HPG + en
        e_ref[d] = ((ek == col) | (ek == col + 64) | (ek == col + LANES)).astype(BF16)

    d_row = d_ref[...]

    def front(c, rev):
        s0 = pl.multiple_of(c * q, q)
        xs_b = x_ref[0, pl.ds(s0, q), :]
        bm_b = b_ref[0, pl.ds(s0, q), :]
        cm_b = c_ref[0, pl.ds(s0, q), :]
        gmat = _dot_nt(cm_b, bm_b)
        bm_t = bm_b.astype(F32).T
        keep = (cols >= rows) if rev else (cols <= rows)
        grp_rows = pl.ds(pl.multiple_of(g * 8, 8), 8)
        return dict(xs_b=xs_b, cm_b=cm_b, gmat=gmat, bm_t=bm_t, keep=keep,
                    dt_rows=dtr_ref[grp_rows, pl.ds(s0, q)],
                    cs_rows=csr_ref[grp_rows, pl.ds(s0, q)],
                    cs_cols=csc_ref[0, pl.ds(s0, q), :])

    def bcast_lhs(f):
        hi, mid, lo = _split3(f["cs_cols"])
        return jnp.concatenate([jnp.where(lane < 64, hi, mid), lo], axis=1)

    def middle(f, bc, rev):
        d = 1 if rev else 0
        csx = jnp.concatenate(
            [jnp.where(lane < 64, bc[:, 0:q], bc[:, q:2 * q]),
             jnp.where(lane < 64, bc[:, 2 * q:3 * q], bc[:, 3 * q:4 * q])], axis=1)
        mmats, bts, xms = [], [], []
        for r in range(SSD_HPG):
            row = d * SSD_HPG + r
            cs_r = f["cs_rows"][row:row + 1, :]
            dt_r = f["dt_rows"][row:row + 1, :]
            end_r = cs_r[:, 0:1] if rev else cs_r[:, q - 1:q]
            seg = bc[:, r * q:(r + 1) * q] - cs_r
            lmat = jnp.exp(jnp.minimum(seg, 0.0))
            mmats.append(jnp.where(f["keep"], f["gmat"] * lmat * dt_r, 0.0).astype(BF16))
            xms.append(jnp.where((lane_gs >> 6) == r, f["xs_b"], jnp.zeros_like(f["xs_b"])))
            w_r = jnp.exp(end_r - cs_r) * dt_r
            bts.append((f["bm_t"] * w_r).astype(BF16))
        pairs = [(jnp.concatenate([jnp.concatenate(mmats[r:r + 2], axis=1),
                                   jnp.concatenate(bts[r:r + 2], axis=1)], axis=0),
                  jnp.concatenate(xms[r:r + 2], axis=0)) for r in range(0, SSD_HPG, 2)]
        return dict(cm_b=f["cm_b"], xs_b=f["xs_b"], decay=jnp.exp(csx),
                    end_decay=jnp.exp(csx[0:1, :] if rev else csx[q - 1:q, :]), pairs=pairs)

    def back(m, rev):
        st_ref = stb_ref if rev else stf_ref
        state = st_ref[...]
        y = _dot(m["cm_b"], state.astype(BF16)) * m["decay"]
        dstate = jnp.zeros((SSD_D_STATE, GS), F32)
        for lhs, xpair in m["pairs"]:
            both = _dot(lhs, xpair)
            y = y + both[0:q, :]
            dstate = dstate + both[q:, :]
        st_ref[...] = state * m["end_decay"] + dstate
        return y

    depth = SCAN_DEPTH

    def step(i, finish):
        work = ([(depth * i + k, False) for k in range(depth)]
                + [(nchunk - 1 - depth * i - k, True) for k in range(depth)])
        fronts = [front(c, rev) for c, rev in work]
        bcs = []
        for d in range(2):
            mine = fronts[d * depth:(d + 1) * depth]
            both = _dot(jnp.concatenate([bcast_lhs(f) for f in mine], axis=0), e_ref[d])
            bcs += [both[k * q:(k + 1) * q, :] for k in range(depth)]
        mids = [middle(f, bc, rev) for f, bc, (_, rev) in zip(fronts, bcs, work)]
        for k in [j + d * depth for j in range(depth) for d in range(2)]:
            m, (c, rev) = mids[k], work[k]
            y = back(m, rev)
            if not rev:
                y = y + m["xs_b"].astype(F32) * d_row
            rs = pl.ds(pl.multiple_of(c * q, q), q)
            if finish:
                y_ref[0, rs, :] = (acc_ref[rs, :] + y).astype(y_ref.dtype)
            else:
                acc_ref[rs, :] = y

    stf_ref[...] = jnp.zeros_like(stf_ref)
    stb_ref[...] = jnp.zeros_like(stb_ref)

    def first_half(i, carry):
        step(i, False)
        return carry

    def second_half(i, carry):
        step(i, True)
        return carry

    steps = nchunk // depth
    lax.fori_loop(0, steps // 2, first_half, 0)
    lax.fori_loop(steps // 2, steps, second_half, 0)


def _ssd_scan(x3, b3, c3, csc3, dtr, csr, d_exp):
    bsz, seq, _ = x3.shape
    assert seq % (2 * SCAN_DEPTH * SSD_CHUNK) == 0
    grp = lambda n: pl.BlockSpec((1, seq, n), lambda b, g: (b, 0, g))
    heads_rows = pl.BlockSpec((2 * SSD_HEADS, seq), lambda b, g: (0, b))
    return pl.pallas_call(
        functools.partial(_ssd_scan_kernel, seq=seq),
        name="ssd_scan",
        grid=(bsz, SSD_GROUPS),
        in_specs=[grp(GS), grp(SSD_D_STATE), grp(SSD_D_STATE),
                  pl.BlockSpec((1, seq, LANES), lambda b, g: (b, 0, 0)),
                  heads_rows, heads_rows,
                  pl.BlockSpec((1, GS), lambda b, g: (0, g))],
        out_specs=pl.BlockSpec((1, seq, GS), lambda b, g: (b, 0, g)),
        out_shape=jax.ShapeDtypeStruct((bsz, seq, SSD_D_INNER), BF16),
        scratch_shapes=[pltpu.VMEM((seq, GS), F32),
                        pltpu.VMEM((SSD_D_STATE, GS), F32),
                        pltpu.VMEM((SSD_D_STATE, GS), F32),
                        pltpu.VMEM((2, 2 * LANES, SSD_HPG * SSD_CHUNK), BF16)],
        compiler_params=pltpu.CompilerParams(dimension_semantics=("arbitrary", "arbitrary"),
                                             vmem_limit_bytes=VMEM_LIMIT),
    )(x3, b3, c3, csc3, dtr, csr, d_exp)


def _ssd_out_kernel(y_ref, z_ref, nw_ref, w_ref, h_ref, o_ref):
    u = y_ref[...].astype(F32) * _silu(z_ref[...].astype(F32))
    un = _rms_rows(u, nw_ref[...]).astype(BF16)
    o_ref[...] = h_ref[...] + _dot(un, w_ref[...])


def _ssd_out_proj(y2, z2, nw, w, h2):
    t = h2.shape[0]
    tm = min(ROW_TILE, t)
    row = lambda n: pl.BlockSpec((tm, n), lambda i: (i, 0))
    return pl.pallas_call(
        _ssd_out_kernel,
        name="ssd_out_proj",
        grid=(t // tm,),
        in_specs=[row(SSD_D_INNER), row(SSD_D_INNER), _resident(nw.shape), _resident(w.shape),
                  row(D_MODEL)],
        out_specs=row(D_MODEL),
        out_shape=jax.ShapeDtypeStruct((t, D_MODEL), F32),
        compiler_params=pltpu.CompilerParams(dimension_semantics=("arbitrary",),
                                             vmem_limit_bytes=VMEM_LIMIT),
    )(y2, z2, nw, w, h2)


def _attn_in_kernel(h_ref, nw_ref, wqt_ref, wk_ref, wvt_ref, wz_ref, qn_ref, kn_ref, ones_ref,
                    qt_ref, k_ref, vt_ref, z_ref):
    xn = _rms_rows(h_ref[...], nw_ref[...]).astype(BF16)
    inv_d = 1.0 / ATTN_HEAD_DIM

    tm = h_ref.shape[0]
    yq = _dot_nt(wqt_ref[...], xn).reshape(-1, ATTN_HEAD_DIM, tm)
    ssq = jnp.sum(yq * yq, axis=1, keepdims=True)
    qn = (yq * lax.rsqrt(ssq * inv_d + RMS_EPS)).reshape(ATTN_QK_DIM, tm)
    qt_ref[...] = (qn * qn_ref[...]).astype(BF16)

    yk = _dot(xn, wk_ref[...])
    ssk = _dot((yk * yk).astype(BF16), ones_ref[...])
    k_ref[...] = (yk * lax.rsqrt(ssk * inv_d + RMS_EPS) * kn_ref[...]).astype(BF16)

    vt_ref[...] = _dot_nt(wvt_ref[...], xn).astype(BF16)
    z_ref[...] = _dot(xn, wz_ref[...]).astype(BF16)


def _attn_in_proj(h2, nw, wqt, wk, wvt, wz, qn_cols, kn, ones_bd):
    t = h2.shape[0]
    tm = qn_cols.shape[1]
    row = lambda n: pl.BlockSpec((tm, n), lambda i: (i, 0))
    col = lambda n: pl.BlockSpec((n, tm), lambda i: (0, i))
    return pl.pallas_call(
        _attn_in_kernel,
        name="attn_in_proj",
        grid=(t // tm,),
        in_specs=[row(D_MODEL)] + [_resident(a.shape)
                                   for a in (nw, wqt, wk, wvt, wz, qn_cols, kn, ones_bd)],
        out_specs=[col(ATTN_QK_DIM), row(ATTN_QK_DIM), col(ATTN_V_DIM), row(ATTN_V_DIM)],
        out_shape=[jax.ShapeDtypeStruct((ATTN_QK_DIM, t), BF16),
                   jax.ShapeDtypeStruct((t, ATTN_QK_DIM), BF16),
                   jax.ShapeDtypeStruct((ATTN_V_DIM, t), BF16),
                   jax.ShapeDtypeStruct((t, ATTN_V_DIM), BF16)],
        compiler_params=pltpu.CompilerParams(dimension_semantics=("arbitrary",),
                                             vmem_limit_bytes=VMEM_LIMIT),
    )(h2, nw, wqt, wk, wvt, wz, qn_cols, kn, ones_bd)


def _attn_kernel(thr_ref, rb_ref, qt_ref, k_ref, vt_ref, z_ref, lam_ref, subln_ref, o_ref,
                 bias_ref, sa_ref, sb_ref, p_ref, m_ref, acc_ref, vte_ref, *, seq, tile, ktile,
                 lam_init):
    h = pl.program_id(0)
    b = pl.program_id(1)
    qi = pl.program_id(2)
    nk = seq // ktile
    kfac = ktile // tile
    half_buckets = REL_BUCKETS // 2
    sub = 8

    @pl.when((b == 0) & (qi == 0))
    def _build_bias():
        rel0 = (lax.broadcasted_iota(jnp.int32, (ktile, tile), 0)
                - lax.broadcasted_iota(jnp.int32, (ktile, tile), 1))
        for idx in range(kfac + 2):
            rel = rel0 + (idx - kfac) * tile
            n = jnp.abs(rel)
            neg = jnp.full((ktile, tile), rb_ref[h], F32)
            pos = jnp.full((ktile, tile), rb_ref[half_buckets * ATTN_HEADS + h], F32)
            for t in range(1, half_buckets):
                ge = n >= thr_ref[t]
                neg = jnp.where(ge, rb_ref[t * ATTN_HEADS + h], neg)
                pos = jnp.where(ge, rb_ref[(half_buckets + t) * ATTN_HEADS + h], pos)
            bias_ref[idx] = jnp.where(rel > 0, pos, neg) * LOG2E

    @pl.when(qi == 0)
    def _extend_v():
        vte_ref[0:ATTN_V_HEAD, :] = vt_ref[...]
        vte_ref[ATTN_V_HEAD:, :] = jnp.ones((ATTN_ONES_ROWS, seq), BF16)

    c_left = rb_ref[(half_buckets - 1) * ATTN_HEADS + h] * LOG2E
    c_right = rb_ref[(2 * half_buckets - 1) * ATTN_HEADS + h] * LOG2E

    def offset(kc):
        return kfac * kc - qi

    def is_near(kc):
        return (offset(kc) >= -kfac) & (offset(kc) <= 1)

    def far_bias(kc):
        return jnp.where(is_near(kc), 0.0, jnp.where(offset(kc) < 0, c_left, c_right))

    q_t = qt_ref[...]
    row = lax.broadcasted_iota(jnp.int32, (ATTN_V_HEAD, tile), 0)
    zero = jnp.zeros_like(q_t)
    qs = (jnp.where(row < ATTN_HEAD_DIM, q_t, zero), jnp.where(row >= ATTN_HEAD_DIM, q_t, zero))

    m_ref[...] = jnp.full(m_ref.shape, -jnp.inf, F32)
    acc_ref[...] = jnp.zeros(acc_ref.shape, F32)

    def scores(kc, t, s_out, with_bias):
        kt = k_ref[0, pl.ds(pl.multiple_of(kc * ktile, ktile), ktile), :]
        sb = _dot(kt, qs[t])
        if with_bias:
            sb = sb + bias_ref[jnp.clip(offset(kc) + kfac, 0, kfac + 1)]
        s_out[t] = sb
        m8 = sb[0:sub, :]
        for i in range(1, ktile // sub):
            m8 = jnp.maximum(m8, sb[i * sub:(i + 1) * sub, :])
        return m8

    def step_body(kc_next, s_next, kc, s_cur, m8s, with_bias):
        vt = vte_ref[:, pl.ds(pl.multiple_of(kc * ktile, ktile), ktile)]
        shift = far_bias(kc)
        m8s_next = []
        for t in range(2):
            m_old = m_ref[t]
            m_new = jnp.maximum(m_old, jnp.max(m8s[t], axis=0, keepdims=True) + shift)
            alpha = jnp.exp2(m_old - m_new)
            msub = m_new - shift
            m8s_next.append(scores(kc_next, t, s_next, with_bias))
            for rb in range(ktile // ATTN_ROWS_B):
                rows = pl.ds(rb * ATTN_ROWS_B, ATTN_ROWS_B)
                p_ref[t, rows, :] = jnp.exp2(s_cur[t, rows, :] - msub).astype(BF16)
            acc_ref[t] = alpha * acc_ref[t] + _dot(vt, p_ref[t])
            m_ref[t] = m_new
        return tuple(m8s_next)

    def step(kc_next, s_next, kc, s_cur, m8s):
        return lax.cond(is_near(kc_next),
                        lambda: step_body(kc_next, s_next, kc, s_cur, m8s, True),
                        lambda: step_body(kc_next, s_next, kc, s_cur, m8s, False))

    def chunk_pair(j, m8s):
        c0 = 2 * j
        m8s_odd = step(c0 + 1, sb_ref, c0, sa_ref, m8s)
        return step(jnp.minimum(c0 + 2, nk - 1), sa_ref, c0 + 1, sb_ref, m8s_odd)

    def first_scores(with_bias):
        def fn():
            return tuple(scores(0, t, sa_ref, with_bias) for t in range(2))
        return fn

    lax.fori_loop(0, nk // 2, chunk_pair,
                  lax.cond(is_near(0), first_scores(True), first_scores(False)))

    lv = lam_ref[...]
    lam = (jnp.exp(jnp.sum(lv[0:1] * lv[1:2], axis=-1, keepdims=True))
           - jnp.exp(jnp.sum(lv[2:3] * lv[3:4], axis=-1, keepdims=True)) + lam_init)
    o1 = acc_ref[0, 0:ATTN_V_HEAD, :] / acc_ref[0, ATTN_V_HEAD:ATTN_V_HEAD + 1, :]
    o2 = acc_ref[1, 0:ATTN_V_HEAD, :] / acc_ref[1, ATTN_V_HEAD:ATTN_V_HEAD + 1, :]
    o = (o1 - lam * o2).T
    o = _rms_rows(o, subln_ref[...]) * (1.0 - lam_init)
    o_ref[0] = (o * _silu(z_ref[0].astype(F32))).astype(o_ref.dtype)


def _attention(thr, rb_flat, qt, k3, vt, z3, lam_vec, subln, lam_init):
    bsz, seq, _ = k3.shape
    tile = min(ATTN_TILE, seq)
    ktile = min(ATTN_KTILE, seq // 2)
    nq = seq // tile
    assert tile > REL_MAX_DIST and ktile % tile == 0 and seq % (2 * ktile) == 0
    assert ktile % ATTN_ROWS_B == 0
    smem = pl.BlockSpec(memory_space=pltpu.SMEM)
    qspec = pl.BlockSpec((ATTN_V_HEAD, tile), lambda h, b, i: (h, b * nq + i))
    kspec = pl.BlockSpec((1, seq, ATTN_V_HEAD), lambda h, b, i: (b, 0, h))
    vspec = pl.BlockSpec((ATTN_V_HEAD, seq), lambda h, b, i: (h, b))
    zspec = pl.BlockSpec((1, tile, ATTN_V_HEAD), lambda h, b, i: (b, i, h))
    return pl.pallas_call(
        functools.partial(_attn_kernel, seq=seq, tile=tile, ktile=ktile, lam_init=lam_init),
        name="diff_attention",
        grid=(ATTN_HEADS, bsz, nq),
        in_specs=[smem, smem, qspec, kspec, vspec, zspec,
                  pl.BlockSpec(lam_vec.shape, lambda h, b, i: (0, 0)),
                  pl.BlockSpec(subln.shape, lambda h, b, i: (0, 0))],
        out_specs=zspec,
        out_shape=jax.ShapeDtypeStruct((bsz, seq, ATTN_V_DIM), BF16),
        scratch_shapes=[pltpu.VMEM((ktile // tile + 2, ktile, tile), F32),
                        pltpu.VMEM((2, ktile, tile), F32),
                        pltpu.VMEM((2, ktile, tile), F32),
                        pltpu.VMEM((2, ktile, tile), BF16),
                        pltpu.VMEM((2, 1, tile), F32),
                        pltpu.VMEM((2, ATTN_V_HEAD + ATTN_ONES_ROWS, tile), F32),
                        pltpu.VMEM((ATTN_V_HEAD + ATTN_ONES_ROWS, seq), BF16)],
        compiler_params=pltpu.CompilerParams(
            dimension_semantics=("arbitrary", "arbitrary", "arbitrary"),
            vmem_limit_bytes=VMEM_LIMIT),
    )(thr, rb_flat, qt, k3, vt, z3, lam_vec, subln)


def _attn_out_kernel(o_ref, w_ref, h_ref, out_ref):
    out_ref[...] = h_ref[...] + _dot(o_ref[...], w_ref[...])


def _attn_out_proj(o2, w, h2):
    t = h2.shape[0]
    tm = min(ROW_TILE, t)
    row = lambda n: pl.BlockSpec((tm, n), lambda i: (i, 0))
    return pl.pallas_call(
        _attn_out_kernel,
        name="attn_out_proj",
        grid=(t // tm,),
        in_specs=[row(ATTN_V_DIM), _resident(w.shape), row(D_MODEL)],
        out_specs=row(D_MODEL),
        out_shape=jax.ShapeDtypeStruct((t, D_MODEL), F32),
        compiler_params=pltpu.CompilerParams(dimension_semantics=("arbitrary",),
                                             vmem_limit_bytes=VMEM_LIMIT),
    )(o2, w, h2)


def _ssd_layer(h2, bsz, seq, nw, w_in, conv_w, conv_b, dt_bias, a_log, d_skip, norm_w, w_out):
    wb16 = w_in.astype(BF16)
    o_x = SSD_D_INNER
    o_b = 2 * SSD_D_INNER
    o_c = o_b + SSD_GN
    o_dt = o_c + SSD_GN
    wdt = wb16[:, o_dt:]
    wdc = jnp.concatenate([wdt, wdt], axis=1)
    perm = lambda a: a.reshape(2, SSD_GROUPS, SSD_HPG).transpose(1, 0, 2).reshape(2 * SSD_HEADS)
    wdr = wdt.T.reshape(2, SSD_GROUPS, SSD_HPG, D_MODEL).transpose(1, 0, 2, 3).reshape(
        2 * SSD_HEADS, D_MODEL)
    del o_b, o_c
    tm = min(ROW_TILE, seq)
    flat = lambda a: a.reshape(2 * SSD_HEADS).astype(F32)
    biasc = jnp.tile(flat(dt_bias), 2).reshape(1, LANES)
    alogc = jnp.tile(flat(a_log), 2).reshape(1, LANES)
    biasr = jnp.broadcast_to(perm(flat(dt_bias))[:, None], (2 * SSD_HEADS, tm))
    alogr = jnp.broadcast_to(perm(flat(a_log))[:, None], (2 * SSD_HEADS, tm))
    d_exp = jnp.repeat(d_skip.astype(F32), SSD_HEADDIM).reshape(1, SSD_D_INNER)
    z2, x2, b2, c2, csc, dtr, csr = _ssd_in_proj(
        h2, seq, nw.reshape(1, -1), wb16[:, :o_x], wb16[:, o_x:o_dt], wdc, wdr,
        conv_w.astype(F32), conv_b.reshape(1, -1).astype(F32), biasc, alogc, biasr, alogr)

    y3 = _ssd_scan(x2.reshape(bsz, seq, -1), b2.reshape(bsz, seq, -1), c2.reshape(bsz, seq, -1),
                   csc.reshape(bsz, seq, LANES), dtr, csr, d_exp)
    return _ssd_out_proj(y3.reshape(bsz * seq, -1), z2, norm_w.reshape(1, -1),
                         w_out.astype(BF16), h2)


def _t5_thresholds():
    nb = REL_BUCKETS // 2
    max_exact = nb // 2
    n = jnp.arange(REL_MAX_DIST + 1, dtype=jnp.int32)
    nf = jnp.maximum(n, 1).astype(F32)
    large = max_exact + (jnp.log(nf / max_exact) / math.log(REL_MAX_DIST / max_exact)
                         * (nb - max_exact)).astype(jnp.int32)
    bucket = jnp.where(n < max_exact, n, jnp.minimum(large, nb - 1))
    t = jnp.arange(nb, dtype=jnp.int32)
    return jnp.sum((bucket[None, :] < t[:, None]).astype(jnp.int32), axis=1)


def _attn_layer(h2, bsz, seq, nw, w_in, q_norm, k_norm, lam_vec, subln, w_out, thr, rb_flat,
                ones_bd, layer_idx):
    wb16 = w_in.astype(BF16)
    reps = ATTN_QK_DIM // ATTN_HEAD_DIM
    tm = min(ROW_TILE, bsz * seq)
    qn = jnp.tile(q_norm.astype(F32), reps) * (ATTN_HEAD_DIM ** -0.5 * LOG2E)
    qn_cols = jnp.broadcast_to(qn[:, None], (ATTN_QK_DIM, tm))
    kn = jnp.tile(k_norm.astype(F32), reps).reshape(1, -1)
    qt, k2, vt, z2 = _attn_in_proj(
        h2, nw.reshape(1, -1), wb16[:, :ATTN_QK_DIM].T, wb16[:, ATTN_QK_DIM:2 * ATTN_QK_DIM],
        wb16[:, 2 * ATTN_QK_DIM:2 * ATTN_QK_DIM + ATTN_V_DIM].T,
        wb16[:, 2 * ATTN_QK_DIM + ATTN_V_DIM:], qn_cols, kn, ones_bd)
    lam_init = 0.8 - 0.6 * math.exp(-0.3 * layer_idx)
    r3 = lambda a: a.reshape(bsz, seq, -1)
    o3 = _attention(thr, rb_flat, qt, r3(k2), vt, r3(z2), lam_vec.astype(F32),
                    subln.reshape(1, -1).astype(F32), lam_init)
    return _attn_out_proj(o3.reshape(bsz * seq, -1), w_out.astype(BF16), h2)


def kernel(x, norm_w, ssd_w_in, ssd_conv_w, ssd_conv_b, ssd_dt_bias, ssd_a_log, ssd_d, ssd_norm_w, ssd_w_out, attn_w_in, attn_q_norm, attn_k_norm, attn_lambda, attn_subln, attn_w_out, rel_bias):
    bsz, seq, _ = x.shape
    depth = norm_w.shape[0]
    h2 = x.reshape(bsz * seq, D_MODEL)
    thr = _t5_thresholds()
    rb_flat = rel_bias.astype(F32).reshape(-1)
    blk = jnp.arange(ATTN_QK_DIM, dtype=jnp.int32) // ATTN_HEAD_DIM
    ones_bd = (blk[:, None] == blk[None, :]).astype(BF16)
    for i in range(depth):
        j = i // 2
        if i % 2 == 0:
            h2 = _ssd_layer(h2, bsz, seq, norm_w[i], ssd_w_in[j], ssd_conv_w[j], ssd_conv_b[j],
                            ssd_dt_bias[j], ssd_a_log[j], ssd_d[j], ssd_norm_w[j], ssd_w_out[j])
        else:
            h2 = _attn_layer(h2, bsz, seq, norm_w[i], attn_w_in[j], attn_q_norm[j], attn_k_norm[j],
                             attn_lambda[j], attn_subln[j], attn_w_out[j], thr, rb_flat, ones_bd, i)
    return h2.reshape(bsz, seq, D_MODEL)
```

```python
import functools
import math

import jax
import jax.numpy as jnp
from jax import lax
from jax.experimental import pallas as pl
from jax.experimental.pallas import tpu as pltpu

F32 = jnp.float32
BF16 = jnp.bfloat16

RMS_EPS = 1e-6

D_MODEL = 1024
SSD_D_INNER = 2048
SSD_HEADDIM = 64
SSD_HEADS = 32
SSD_D_STATE = 128
SSD_GROUPS = 8
SSD_HPG = 4
SSD_CONV = 5
SSD_CHUNK = 128
SSD_GN = SSD_GROUPS * SSD_D_STATE
ATTN_HEADS = 8
ATTN_HEAD_DIM = 64
ATTN_V_HEAD = 128
ATTN_QK_DIM = 1024
ATTN_V_DIM = 1024
REL_BUCKETS = 32
REL_MAX_DIST = 128

V7X_VMEM_BYTES = 64 * 1024 * 1024
VMEM_LIMIT = V7X_VMEM_BYTES - 8 * 1024 * 1024
LANES = 128
BF16_SUBLANES = 16

ROW_TILE = 512
HALO = BF16_SUBLANES
CONV_COLS = 512
ATTN_TILE = 512
ATTN_KTILE = 1024
ATTN_ROWS_B = 64
ATTN_ONES_ROWS = BF16_SUBLANES
LOG2E = 1.4426950408889634
GS = SSD_HPG * SSD_HEADDIM
SCAN_DEPTH = 2


def _resident(shape):
    nd = len(shape)
    return pl.BlockSpec(shape, lambda *_: (0,) * nd, pipeline_mode=pl.Buffered(1))


def _silu(x):
    return x * (1.0 / (1.0 + jnp.exp(-x)))


def _softplus(x):
    return jnp.maximum(x, 0.0) + jnp.log(1.0 + jnp.exp(-jnp.abs(x)))


def _rms_rows(x, w_row):
    ms = jnp.mean(x * x, axis=-1, keepdims=True)
    return x * lax.rsqrt(ms + RMS_EPS) * w_row


def _dot(a, b):
    return jnp.dot(a, b, preferred_element_type=F32)


def _dot_nt(a, b):
    return lax.dot_general(a, b, (((1,), (1,)), ((), ())), preferred_element_type=F32)


def _split3(x):
    hi = x.astype(BF16)
    r1 = x - hi.astype(F32)
    mid = r1.astype(BF16)
    lo = (r1 - mid.astype(F32)).astype(BF16)
    return hi, mid, lo


def _ssd_in_kernel(h_ref, hp_ref, hn_ref, nw_ref, wz_ref, wxbc_ref, wdc_ref, wdr_ref,
                   cw_ref, cb_ref, biasc_ref, alogc_ref, biasr_ref, alogr_ref,
                   z_ref, x_ref, b_ref, c_ref, csc_ref, dtr_ref, csr_ref, pre_ref, *, seq):
    i = pl.program_id(0)
    tm = h_ref.shape[0]
    q = SSD_CHUNK
    nw = nw_ref[...]
    at_start = (i * tm) % seq == 0
    at_end = ((i + 1) * tm) % seq == 0
    xn = _rms_rows(h_ref[...], nw).astype(BF16)
    xp = (_rms_rows(hp_ref[...], nw) * jnp.where(at_start, 0.0, 1.0)).astype(BF16)
    xq = (_rms_rows(hn_ref[...], nw) * jnp.where(at_end, 0.0, 1.0)).astype(BF16)
    ext = jnp.concatenate([xp, xn, xq], axis=0)

    z_ref[...] = _dot(xn, wz_ref[...]).astype(BF16)

    rows = lax.broadcasted_iota(jnp.int32, (q, q), 0)
    cols = lax.broadcasted_iota(jnp.int32, (q, q), 1)
    lower_b = (cols <= rows).astype(BF16)
    upper_f = (rows <= cols).astype(F32)
    lower_f = (rows >= cols).astype(F32)
    a_cols = _softplus(_dot(xn, wdc_ref[...]) + biasc_ref[...]) * -jnp.exp(alogc_ref[...])
    bwd_lane = (lax.broadcasted_iota(jnp.int32, (q, LANES), 1) & (SSD_HEADS)) != 0
    dt_rows = _softplus(_dot_nt(wdr_ref[...], xn) + biasr_ref[...])
    a_rows = dt_rows * -jnp.exp(alogr_ref[...])
    dtr_ref[...] = dt_rows
    bwd_row = (lax.broadcasted_iota(jnp.int32, (2 * SSD_HEADS, q), 0) & SSD_HPG) != 0
    for j in range(tm // q):
        blk = a_cols[j * q:(j + 1) * q, :]
        hi, mid, lo = _split3(blk)
        r3 = _dot(lower_b, jnp.concatenate([hi, mid, lo], axis=1))
        pre = r3[:, :LANES] + r3[:, LANES:2 * LANES] + r3[:, 2 * LANES:]
        suf = pre[q - 1:q, :] - pre + blk
        csc_ref[j * q:(j + 1) * q, :] = jnp.where(bwd_lane, suf, pre)
        rblk = a_rows[:, j * q:(j + 1) * q]
        pre_r = jnp.dot(rblk, upper_f, preferred_element_type=F32, precision=lax.Precision.HIGHEST)
        suf_r = jnp.dot(rblk, lower_f, preferred_element_type=F32, precision=lax.Precision.HIGHEST)
        csr_ref[:, j * q:(j + 1) * q] = jnp.where(bwd_row, suf_r, pre_r)

    half = SSD_CONV // 2
    outs = ((x_ref, 0, SSD_D_INNER), (b_ref, SSD_D_INNER, SSD_GN),
            (c_ref, SSD_D_INNER + SSD_GN, SSD_GN))
    for o_ref, base, width in outs:
        for j in range(width // CONV_COLS):
            lo = base + j * CONV_COLS
            pre_ref[...] = _dot(ext, wxbc_ref[:, lo:lo + CONV_COLS])
            w = cw_ref[:, lo:lo + CONV_COLS]
            acc = cb_ref[:, lo:lo + CONV_COLS]
            for k in range(SSD_CONV):
                off = HALO - half + k
                acc = acc + pre_ref[off:off + tm, :] * w[k:k + 1, :]
            o_ref[:, j * CONV_COLS:(j + 1) * CONV_COLS] = _silu(acc).astype(BF16)


def _ssd_in_proj(h2, seq, nw, wz, wxbc, wdc, wdr, conv_w, conv_b, biasc, alogc, biasr, alogr):
    t = h2.shape[0]
    tm = min(ROW_TILE, seq)
    assert seq % tm == 0 and tm % HALO == 0 and tm % SSD_CHUNK == 0 and biasr.shape[1] == tm
    per = tm // HALO
    last = t // HALO - 1
    row = lambda n: pl.BlockSpec((tm, n), lambda i: (i, 0))
    return pl.pallas_call(
        functools.partial(_ssd_in_kernel, seq=seq),
        name="ssd_in_proj",
        grid=(t // tm,),
        in_specs=[row(D_MODEL),
                  pl.BlockSpec((HALO, D_MODEL), lambda i: (jnp.maximum(i * per - 1, 0), 0)),
                  pl.BlockSpec((HALO, D_MODEL), lambda i: (jnp.minimum((i + 1) * per, last), 0)),
                  _resident(nw.shape), _resident(wz.shape), _resident(wxbc.shape),
                  _resident(wdc.shape), _resident(wdr.shape), _resident(conv_w.shape),
                  _resident(conv_b.shape), _resident(biasc.shape), _resident(alogc.shape),
                  _resident(biasr.shape), _resident(alogr.shape)],
        out_specs=[row(SSD_D_INNER), row(SSD_D_INNER), row(SSD_GN), row(SSD_GN), row(LANES),
                   pl.BlockSpec((2 * SSD_HEADS, tm), lambda i: (0, i)),
                   pl.BlockSpec((2 * SSD_HEADS, tm), lambda i: (0, i))],
        out_shape=[jax.ShapeDtypeStruct((t, SSD_D_INNER), BF16),
                   jax.ShapeDtypeStruct((t, SSD_D_INNER), BF16),
                   jax.ShapeDtypeStruct((t, SSD_GN), BF16),
                   jax.ShapeDtypeStruct((t, SSD_GN), BF16),
                   jax.ShapeDtypeStruct((t, LANES), F32),
                   jax.ShapeDtypeStruct((2 * SSD_HEADS, t), F32),
                   jax.ShapeDtypeStruct((2 * SSD_HEADS, t), F32)],
        scratch_shapes=[pltpu.VMEM((tm + 2 * HALO, CONV_COLS), F32)],
        compiler_params=pltpu.CompilerParams(dimension_semantics=("arbitrary",),
                                             vmem_limit_bytes=VMEM_LIMIT),
    )(h2, h2, h2, nw, wz, wxbc, wdc, wdr, conv_w, conv_b, biasc, alogc, biasr, alogr)


def _ssd_scan_kernel(x_ref, b_ref, c_ref, csc_ref, dtr_ref, csr_ref, d_ref,
                     y_ref, acc_ref, stf_ref, stb_ref, e_ref, *, seq):
    g = pl.program_id(1)
    q = SSD_CHUNK
    nchunk = seq // q

    rows = lax.broadcasted_iota(jnp.int32, (q, q), 0)
    cols = lax.broadcasted_iota(jnp.int32, (q, q), 1)
    lane = lax.broadcasted_iota(jnp.int32, (q, LANES), 1)
    lane_gs = lax.broadcasted_iota(jnp.int32, (q, GS), 1)

    ek = lax.broadcasted_iota(jnp.int32, (2 * LANES, SSD_HPG * q), 0)
    en = lax.broadcasted_iota(jnp.int32, (2 * LANES, SSD_HPG * q), 1) >> 7
    for d in range(2):
        col = d * SSD_HEADS + g * SSD_HPG + en
        e_ref[d] = ((ek == col) | (ek == col + 64) | (ek == col + LANES)).astype(BF16)

    d_row = d_ref[...]

    def front(c, rev):
        s0 = pl.multiple_of(c * q, q)
        xs_b = x_ref[0, pl.ds(s0, q), :]
        bm_b = b_ref[0, pl.ds(s0, q), :]
        cm_b = c_ref[0, pl.ds(s0, q), :]
        gmat = _dot_nt(cm_b, bm_b)
        bm_t = bm_b.astype(F32).T
        keep = (cols >= rows) if rev else (cols <= rows)
        grp_rows = pl.ds(pl.multiple_of(g * 8, 8), 8)
        return dict(xs_b=xs_b, cm_b=cm_b, gmat=gmat, bm_t=bm_t, keep=keep,
                    dt_rows=dtr_ref[grp_rows, pl.ds(s0, q)],
                    cs_rows=csr_ref[grp_rows, pl.ds(s0, q)],
                    cs_cols=csc_ref[0, pl.ds(s0, q), :])

    def bcast_lhs(f):
        hi, mid, lo = _split3(f["cs_cols"])
        return jnp.concatenate([jnp.where(lane < 64, hi, mid), lo], axis=1)

    def middle(f, bc, rev):
        d = 1 if rev else 0
        csx = jnp.concatenate(
            [jnp.where(lane < 64, bc[:, 0:q], bc[:, q:2 * q]),
             jnp.where(lane < 64, bc[:, 2 * q:3 * q], bc[:, 3 * q:4 * q])], axis=1)
        mmats, bts, xms = [], [], []
        for r in range(SSD_HPG):
            row = d * SSD_HPG + r
            cs_r = f["cs_rows"][row:row + 1, :]
            dt_r = f["dt_rows"][row:row + 1, :]
            end_r = cs_r[:, 0:1] if rev else cs_r[:, q - 1:q]
            seg = bc[:, r * q:(r + 1) * q] - cs_r
            lmat = jnp.exp(jnp.minimum(seg, 0.0))
            mmats.append(jnp.where(f["keep"], f["gmat"] * lmat * dt_r, 0.0).astype(BF16))
            xms.append(jnp.where((lane_gs >> 6) == r, f["xs_b"], jnp.zeros_like(f["xs_b"])))
            w_r = jnp.exp(end_r - cs_r) * dt_r
            bts.append((f["bm_t"] * w_r).astype(BF16))
        pairs = [(jnp.concatenate([jnp.concatenate(mmats[r:r + 2], axis=1),
                                   jnp.concatenate(bts[r:r + 2], axis=1)], axis=0),
                  jnp.concatenate(xms[r:r + 2], axis=0)) for r in range(0, SSD_HPG, 2)]
        return dict(cm_b=f["cm_b"], xs_b=f["xs_b"], decay=jnp.exp(csx),
                    end_decay=jnp.exp(csx[0:1, :] if rev else csx[q - 1:q, :]), pairs=pairs)

    def back(m, rev):
        st_ref = stb_ref if rev else stf_ref
        state = st_ref[...]
        y = _dot(m["cm_b"], state.astype(BF16)) * m["decay"]
        dstate = jnp.zeros((SSD_D_STATE, GS), F32)
        for lhs, xpair in m["pairs"]:
            both = _dot(lhs, xpair)
            y = y + both[0:q, :]
            dstate = dstate + both[q:, :]
        st_ref[...] = state * m["end_decay"] + dstate
        return y

    depth = SCAN_DEPTH

    def step(i, finish):
        work = ([(depth * i + k, False) for k in range(depth)]
                + [(nchunk - 1 - depth * i - k, True) for k in range(depth)])
        fronts = [front(c, rev) for c, rev in work]
        bcs = []
        for d in range(2):
            mine = fronts[d * depth:(d + 1) * depth]
            both = _dot(jnp.concatenate([bcast_lhs(f) for f in mine], axis=0), e_ref[d])
            bcs += [both[k * q:(k + 1) * q, :] for k in range(depth)]
        mids = [middle(f, bc, rev) for f, bc, (_, rev) in zip(fronts, bcs, work)]
        for k in [j + d * depth for j in range(depth) for d in range(2)]:
            m, (c, rev) = mids[k], work[k]
            y = back(m, rev)
            if not rev:
                y = y + m["xs_b"].astype(F32) * d_row
            rs = pl.ds(pl.multiple_of(c * q, q), q)
            if finish:
                y_ref[0, rs, :] = (acc_ref[rs, :] + y).astype(y_ref.dtype)
            else:
                acc_ref[rs, :] = y

    stf_ref[...] = jnp.zeros_like(stf_ref)
    stb_ref[...] = jnp.zeros_like(stb_ref)

    def first_half(i, carry):
        step(i, False)
        return carry

    def second_half(i, carry):
        step(i, True)
        return carry

    steps = nchunk // depth
    lax.fori_loop(0, steps // 2, first_half, 0)
    lax.fori_loop(steps // 2, steps, second_half, 0)


def _ssd_scan(x3, b3, c3, csc3, dtr, csr, d_exp):
    bsz, seq, _ = x3.shape
    assert seq % (2 * SCAN_DEPTH * SSD_CHUNK) == 0
    grp = lambda n: pl.BlockSpec((1, seq, n), lambda b, g: (b, 0, g))
    heads_rows = pl.BlockSpec((2 * SSD_HEADS, seq), lambda b, g: (0, b))
    return pl.pallas_call(
        functools.partial(_ssd_scan_kernel, seq=seq),
        name="ssd_scan",
        grid=(bsz, SSD_GROUPS),
        in_specs=[grp(GS), grp(SSD_D_STATE), grp(SSD_D_STATE),
                  pl.BlockSpec((1, seq, LANES), lambda b, g: (b, 0, 0)),
                  heads_rows, heads_rows,
                  pl.BlockSpec((1, GS), lambda b, g: (0, g))],
        out_specs=pl.BlockSpec((1, seq, GS), lambda b, g: (b, 0, g)),
        out_shape=jax.ShapeDtypeStruct((bsz, seq, SSD_D_INNER), BF16),
        scratch_shapes=[pltpu.VMEM((seq, GS), F32),
                        pltpu.VMEM((SSD_D_STATE, GS), F32),
                        pltpu.VMEM((SSD_D_STATE, GS), F32),
                        pltpu.VMEM((2, 2 * LANES, SSD_HPG * SSD_CHUNK), BF16)],
        compiler_params=pltpu.CompilerParams(dimension_semantics=("arbitrary", "arbitrary"),
                                             vmem_limit_bytes=VMEM_LIMIT),
    )(x3, b3, c3, csc3, dtr, csr, d_exp)


def _ssd_out_kernel(y_ref, z_ref, nw_ref, w_ref, h_ref, o_ref):
    u = y_ref[...].astype(F32) * _silu(z_ref[...].astype(F32))
    un = _rms_rows(u, nw_ref[...]).astype(BF16)
    o_ref[...] = h_ref[...] + _dot(un, w_ref[...])


def _ssd_out_proj(y2, z2, nw, w, h2):
    t = h2.shape[0]
    tm = min(ROW_TILE, t)
    row = lambda n: pl.BlockSpec((tm, n), lambda i: (i, 0))
    return pl.pallas_call(
        _ssd_out_kernel,
        name="ssd_out_proj",
        grid=(t // tm,),
        in_specs=[row(SSD_D_INNER), row(SSD_D_INNER), _resident(nw.shape), _resident(w.shape),
                  row(D_MODEL)],
        out_specs=row(D_MODEL),
        out_shape=jax.ShapeDtypeStruct((t, D_MODEL), F32),
        compiler_params=pltpu.CompilerParams(dimension_semantics=("arbitrary",),
                                             vmem_limit_bytes=VMEM_LIMIT),
    )(y2, z2, nw, w, h2)


def _attn_in_kernel(h_ref, nw_ref, wqt_ref, wk_ref, wvt_ref, wz_ref, qn_ref, kn_ref, ones_ref,
                    qt_ref, k_ref, vt_ref, z_ref):
    xn = _rms_rows(h_ref[...], nw_ref[...]).astype(BF16)
    inv_d = 1.0 / ATTN_HEAD_DIM

    tm = h_ref.shape[0]
    yq = _dot_nt(wqt_ref[...], xn).reshape(-1, ATTN_HEAD_DIM, tm)
    ssq = jnp.sum(yq * yq, axis=1, keepdims=True)
    qn = (yq * lax.rsqrt(ssq * inv_d + RMS_EPS)).reshape(ATTN_QK_DIM, tm)
    qt_ref[...] = (qn * qn_ref[...]).astype(BF16)

    yk = _dot(xn, wk_ref[...])
    ssk = _dot((yk * yk).astype(BF16), ones_ref[...])
    k_ref[...] = (yk * lax.rsqrt(ssk * inv_d + RMS_EPS) * kn_ref[...]).astype(BF16)

    vt_ref[...] = _dot_nt(wvt_ref[...], xn).astype(BF16)
    z_ref[...] = _dot(xn, wz_ref[...]).astype(BF16)


def _attn_in_proj(h2, nw, wqt, wk, wvt, wz, qn_cols, kn, ones_bd):
    t = h2.shape[0]
    tm = qn_cols.shape[1]
    row = lambda n: pl.BlockSpec((tm, n), lambda i: (i, 0))
    col = lambda n: pl.BlockSpec((n, tm), lambda i: (0, i))
    return pl.pallas_call(
        _attn_in_kernel,
        name="attn_in_proj",
        grid=(t // tm,),
        in_specs=[row(D_MODEL)] + [_resident(a.shape)
                                   for a in (nw, wqt, wk, wvt, wz, qn_cols, kn, ones_bd)],
        out_specs=[col(ATTN_QK_DIM), row(ATTN_QK_DIM), col(ATTN_V_DIM), row(ATTN_V_DIM)],
        out_shape=[jax.ShapeDtypeStruct((ATTN_QK_DIM, t), BF16),
                   jax.ShapeDtypeStruct((t, ATTN_QK_DIM), BF16),
                   jax.ShapeDtypeStruct((ATTN_V_DIM, t), BF16),
                   jax.ShapeDtypeStruct((t, ATTN_V_DIM), BF16)],
        compiler_params=pltpu.CompilerParams(dimension_semantics=("arbitrary",),
                                             vmem_limit_bytes=VMEM_LIMIT),
    )(h2, nw, wqt, wk, wvt, wz, qn_cols, kn, ones_bd)


def _attn_kernel(thr_ref, rb_ref, qt_ref, k_ref, vt_ref, z_ref, lam_ref, subln_ref, o_ref,
                 bias_ref, sa_ref, sb_ref, p_ref, m_ref, acc_ref, vte_ref, *, seq, tile, ktile,
                 lam_init):
    h = pl.program_id(0)
    b = pl.program_id(1)
    qi = pl.program_id(2)
    nk = seq // ktile
    kfac = ktile // tile
    half_buckets = REL_BUCKETS // 2
    sub = 8

    @pl.when((b == 0) & (qi == 0))
    def _build_bias():
        rel0 = (lax.broadcasted_iota(jnp.int32, (ktile, tile), 0)
                - lax.broadcasted_iota(jnp.int32, (ktile, tile), 1))
        for idx in range(kfac + 2):
            rel = rel0 + (idx - kfac) * tile
            n = jnp.abs(rel)
            neg = jnp.full((ktile, tile), rb_ref[h], F32)
            pos = jnp.full((ktile, tile), rb_ref[half_buckets * ATTN_HEADS + h], F32)
            for t in range(1, half_buckets):
                ge = n >= thr_ref[t]
                neg = jnp.where(ge, rb_ref[t * ATTN_HEADS + h], neg)
                pos = jnp.where(ge, rb_ref[(half_buckets + t) * ATTN_HEADS + h], pos)
            bias_ref[idx] = jnp.where(rel > 0, pos, neg) * LOG2E

    @pl.when(qi == 0)
    def _extend_v():
        vte_ref[0:ATTN_V_HEAD, :] = vt_ref[...]
        vte_ref[ATTN_V_HEAD:, :] = jnp.ones((ATTN_ONES_ROWS, seq), BF16)

    c_left = rb_ref[(half_buckets - 1) * ATTN_HEADS + h] * LOG2E
    c_right = rb_ref[(2 * half_buckets - 1) * ATTN_HEADS + h] * LOG2E

    def offset(kc):
        return kfac * kc - qi

    def is_near(kc):
        return (offset(kc) >= -kfac) & (offset(kc) <= 1)

    def far_bias(kc):
        return jnp.where(is_near(kc), 0.0, jnp.where(offset(kc) < 0, c_left, c_right))

    q_t = qt_ref[...]
    row = lax.broadcasted_iota(jnp.int32, (ATTN_V_HEAD, tile), 0)
    zero = jnp.zeros_like(q_t)
    qs = (jnp.where(row < ATTN_HEAD_DIM, q_t, zero), jnp.where(row >= ATTN_HEAD_DIM, q_t, zero))

    m_ref[...] = jnp.full(m_ref.shape, -jnp.inf, F32)
    acc_ref[...] = jnp.zeros(acc_ref.shape, F32)

    def scores(kc, t, s_out, with_bias):
        kt = k_ref[0, pl.ds(pl.multiple_of(kc * ktile, ktile), ktile), :]
        sb = _dot(kt, qs[t])
        if with_bias:
            sb = sb + bias_ref[jnp.clip(offset(kc) + kfac, 0, kfac + 1)]
        s_out[t] = sb
        m8 = sb[0:sub, :]
        for i in range(1, ktile // sub):
            m8 = jnp.maximum(m8, sb[i * sub:(i + 1) * sub, :])
        return m8

    def step_body(kc_next, s_next, kc, s_cur, m8s, with_bias):
        vt = vte_ref[:, pl.ds(pl.multiple_of(kc * ktile, ktile), ktile)]
        shift = far_bias(kc)
        m8s_next = []
        for t in range(2):
            m_old = m_ref[t]
            m_new = jnp.maximum(m_old, jnp.max(m8s[t], axis=0, keepdims=True) + shift)
            alpha = jnp.exp2(m_old - m_new)
            msub = m_new - shift
            if kc_next is not None:
                m8s_next.append(scores(kc_next, t, s_next, with_bias))
            for rb in range(ktile // ATTN_ROWS_B):
                rows = pl.ds(rb * ATTN_ROWS_B, ATTN_ROWS_B)
                p_ref[t, rows, :] = jnp.exp2(s_cur[t, rows, :] - msub).astype(BF16)
            acc_ref[t] = alpha * acc_ref[t] + _dot(vt, p_ref[t])
            m_ref[t] = m_new
        return tuple(m8s_next)

    def step(kc_next, s_next, kc, s_cur, m8s):
        return lax.cond(is_near(kc_next),
                        lambda: step_body(kc_next, s_next, kc, s_cur, m8s, True),
                        lambda: step_body(kc_next, s_next, kc, s_cur, m8s, False))

    def chunk_pair(j, m8s):
        c0 = 2 * j
        m8s_odd = step(c0 + 1, sb_ref, c0, sa_ref, m8s)
        return step(c0 + 2, sa_ref, c0 + 1, sb_ref, m8s_odd)

    def first_scores(with_bias):
        def fn():
            return tuple(scores(0, t, sa_ref, with_bias) for t in range(2))
        return fn

    m8s = lax.fori_loop(0, nk // 2 - 1, chunk_pair,
                        lax.cond(is_near(0), first_scores(True), first_scores(False)))
    m8s = step(nk - 1, sb_ref, nk - 2, sa_ref, m8s)
    step_body(None, None, nk - 1, sb_ref, m8s, False)

    lv = lam_ref[...]
    lam = (jnp.exp(jnp.sum(lv[0:1] * lv[1:2], axis=-1, keepdims=True))
           - jnp.exp(jnp.sum(lv[2:3] * lv[3:4], axis=-1, keepdims=True)) + lam_init)
    o1 = acc_ref[0, 0:ATTN_V_HEAD, :] / acc_ref[0, ATTN_V_HEAD:ATTN_V_HEAD + 1, :]
    o2 = acc_ref[1, 0:ATTN_V_HEAD, :] / acc_ref[1, ATTN_V_HEAD:ATTN_V_HEAD + 1, :]
    o = (o1 - lam * o2).T
    o = _rms_rows(o, subln_ref[...]) * (1.0 - lam_init)
    o_ref[0] = (o * _silu(z_ref[0].astype(F32))).astype(o_ref.dtype)


def _attention(thr, rb_flat, qt, k3, vt, z3, lam_vec, subln, lam_init):
    bsz, seq, _ = k3.shape
    tile = min(ATTN_TILE, seq)
    ktile = min(ATTN_KTILE, seq // 2)
    nq = seq // tile
    assert tile > REL_MAX_DIST and ktile % tile == 0 and seq % (2 * ktile) == 0
    assert ktile % ATTN_ROWS_B == 0
    smem = pl.BlockSpec(memory_space=pltpu.SMEM)
    qspec = pl.BlockSpec((ATTN_V_HEAD, tile), lambda h, b, i: (h, b * nq + i))
    kspec = pl.BlockSpec((1, seq, ATTN_V_HEAD), lambda h, b, i: (b, 0, h))
    vspec = pl.BlockSpec((ATTN_V_HEAD, seq), lambda h, b, i: (h, b))
    zspec = pl.BlockSpec((1, tile, ATTN_V_HEAD), lambda h, b, i: (b, i, h))
    return pl.pallas_call(
        functools.partial(_attn_kernel, seq=seq, tile=tile, ktile=ktile, lam_init=lam_init),
        name="diff_attention",
        grid=(ATTN_HEADS, bsz, nq),
        in_specs=[smem, smem, qspec, kspec, vspec, zspec,
                  pl.BlockSpec(lam_vec.shape, lambda h, b, i: (0, 0)),
                  pl.BlockSpec(subln.shape, lambda h, b, i: (0, 0))],
        out_specs=zspec,
        out_shape=jax.ShapeDtypeStruct((bsz, seq, ATTN_V_DIM), BF16),
        scratch_shapes=[pltpu.VMEM((ktile // tile + 2, ktile, tile), F32),
                        pltpu.VMEM((2, ktile, tile), F32),
                        pltpu.VMEM((2, ktile, tile), F32),
                        pltpu.VMEM((2, ktile, tile), BF16),
                        pltpu.VMEM((2, 1, tile), F32),
                        pltpu.VMEM((2, ATTN_V_HEAD + ATTN_ONES_ROWS, tile), F32),
                        pltpu.VMEM((ATTN_V_HEAD + ATTN_ONES_ROWS, seq), BF16)],
        compiler_params=pltpu.CompilerParams(
            dimension_semantics=("arbitrary", "arbitrary", "arbitrary"),
            vmem_limit_bytes=VMEM_LIMIT),
    )(thr, rb_flat, qt, k3, vt, z3, lam_vec, subln)


def _attn_out_kernel(o_ref, w_ref, h_ref, out_ref):
    out_ref[...] = h_ref[...] + _dot(o_ref[...], w_ref[...])


def _attn_out_proj(o2, w, h2):
    t = h2.shape[0]
    tm = min(ROW_TILE, t)
    row = lambda n: pl.BlockSpec((tm, n), lambda i: (i, 0))
    return pl.pallas_call(
        _attn_out_kernel,
        name="attn_out_proj",
        grid=(t // tm,),
        in_specs=[row(ATTN_V_DIM), _resident(w.shape), row(D_MODEL)],
        out_specs=row(D_MODEL),
        out_shape=jax.ShapeDtypeStruct((t, D_MODEL), F32),
        compiler_params=pltpu.CompilerParams(dimension_semantics=("arbitrary",),
                                             vmem_limit_bytes=VMEM_LIMIT),
    )(o2, w, h2)


def _ssd_layer(h2, bsz, seq, nw, w_in, conv_w, conv_b, dt_bias, a_log, d_skip, norm_w, w_out):
    wb16 = w_in.astype(BF16)
    o_x = SSD_D_INNER
    o_b = 2 * SSD_D_INNER
    o_c = o_b + SSD_GN
    o_dt = o_c + SSD_GN
    wdt = wb16[:, o_dt:]
    wdc = jnp.concatenate([wdt, wdt], axis=1)
    perm = lambda a: a.reshape(2, SSD_GROUPS, SSD_HPG).transpose(1, 0, 2).reshape(2 * SSD_HEADS)
    wdr = wdt.T.reshape(2, SSD_GROUPS, SSD_HPG, D_MODEL).transpose(1, 0, 2, 3).reshape(
        2 * SSD_HEADS, D_MODEL)
    del o_b, o_c
    tm = min(ROW_TILE, seq)
    flat = lambda a: a.reshape(2 * SSD_HEADS).astype(F32)
    biasc = jnp.tile(flat(dt_bias), 2).reshape(1, LANES)
    alogc = jnp.tile(flat(a_log), 2).reshape(1, LANES)
    biasr = jnp.broadcast_to(perm(flat(dt_bias))[:, None], (2 * SSD_HEADS, tm))
    alogr = jnp.broadcast_to(perm(flat(a_log))[:, None], (2 * SSD_HEADS, tm))
    d_exp = jnp.repeat(d_skip.astype(F32), SSD_HEADDIM).reshape(1, SSD_D_INNER)
    z2, x2, b2, c2, csc, dtr, csr = _ssd_in_proj(
        h2, seq, nw.reshape(1, -1), wb16[:, :o_x], wb16[:, o_x:o_dt], wdc, wdr,
        conv_w.astype(F32), conv_b.reshape(1, -1).astype(F32), biasc, alogc, biasr, alogr)

    y3 = _ssd_scan(x2.reshape(bsz, seq, -1), b2.reshape(bsz, seq, -1), c2.reshape(bsz, seq, -1),
                   csc.reshape(bsz, seq, LANES), dtr, csr, d_exp)
    return _ssd_out_proj(y3.reshape(bsz * seq, -1), z2, norm_w.reshape(1, -1),
                         w_out.astype(BF16), h2)


def _t5_thresholds():
    nb = REL_BUCKETS // 2
    max_exact = nb // 2
    n = jnp.arange(REL_MAX_DIST + 1, dtype=jnp.int32)
    nf = jnp.maximum(n, 1).astype(F32)
    large = max_exact + (jnp.log(nf / max_exact) / math.log(REL_MAX_DIST / max_exact)
                         * (nb - max_exact)).astype(jnp.int32)
    bucket = jnp.where(n < max_exact, n, jnp.minimum(large, nb - 1))
    t = jnp.arange(nb, dtype=jnp.int32)
    return jnp.sum((bucket[None, :] < t[:, None]).astype(jnp.int32), axis=1)


def _attn_layer(h2, bsz, seq, nw, w_in, q_norm, k_norm, lam_vec, subln, w_out, thr, rb_flat,
                ones_bd, layer_idx):
    wb16 = w_in.astype(BF16)
    reps = ATTN_QK_DIM // ATTN_HEAD_DIM
    tm = min(ROW_TILE, bsz * seq)
    qn = jnp.tile(q_norm.astype(F32), reps) * (ATTN_HEAD_DIM ** -0.5 * LOG2E)
    qn_cols = jnp.broadcast_to(qn[:, None], (ATTN_QK_DIM, tm))
    kn = jnp.tile(k_norm.astype(F32), reps).reshape(1, -1)
    qt, k2, vt, z2 = _attn_in_proj(
        h2, nw.reshape(1, -1), wb16[:, :ATTN_QK_DIM].T, wb16[:, ATTN_QK_DIM:2 * ATTN_QK_DIM],
        wb16[:, 2 * ATTN_QK_DIM:2 * ATTN_QK_DIM + ATTN_V_DIM].T,
        wb16[:, 2 * ATTN_QK_DIM + ATTN_V_DIM:], qn_cols, kn, ones_bd)
    lam_init = 0.8 - 0.6 * math.exp(-0.3 * layer_idx)
    r3 = lambda a: a.reshape(bsz, seq, -1)
    o3 = _attention(thr, rb_flat, qt, r3(k2), vt, r3(z2), lam_vec.astype(F32),
                    subln.reshape(1, -1).astype(F32), lam_init)
    return _attn_out_proj(o3.reshape(bsz * seq, -1), w_out.astype(BF16), h2)


def kernel(x, norm_w, ssd_w_in, ssd_conv_w, ssd_conv_b, ssd_dt_bias, ssd_a_log, ssd_d, ssd_norm_w, ssd_w_out, attn_w_in, attn_q_norm, attn_k_norm, attn_lambda, attn_subln, attn_w_out, rel_bias):
    bsz, seq, _ = x.shape
    depth = norm_w.shape[0]
    h2 = x.reshape(bsz * seq, D_MODEL)
    thr = _t5_thresholds()
    rb_flat = rel_bias.astype(F32).reshape(-1)
    blk = jnp.arange(ATTN_QK_DIM, dtype=jnp.int32) // ATTN_HEAD_DIM
    ones_bd = (blk[:, None] == blk[None, :]).astype(BF16)
    for i in range(depth):
        j = i // 2
        if i % 2 == 0:
            h2 = _ssd_layer(h2, bsz, seq, norm_w[i], ssd_w_in[j], ssd_conv_w[j], ssd_conv_b[j],
                            ssd_dt_bias[j], ssd_a_log[j], ssd_d[j], ssd_norm_w[j], ssd_w_out[j])
        else:
            h2 = _attn_layer(h2, bsz, seq, norm_w[i], attn_w_in[j], attn_q_norm[j], attn_k_norm[j],
                             attn_lambda[j], attn_subln[j], attn_w_out[j], thr, rb_flat, ones_bd, i)
    return h2.reshape(bsz, seq, D_MODEL)
```

```python
import functools
import math

import jax
import jax.numpy as jnp
from jax import lax
from jax.experimental import pallas as pl
from jax.experimental.pallas import tpu as pltpu

F32 = jnp.float32
BF16 = jnp.bfloat16

RMS_EPS = 1e-6

D_MODEL = 1024
SSD_D_INNER = 2048
SSD_HEADDIM = 64
SSD_HEADS = 32
SSD_D_STATE = 128
SSD_GROUPS = 8
SSD_HPG = 4
SSD_CONV = 5
SSD_CHUNK = 128
SSD_GN = SSD_GROUPS * SSD_D_STATE
ATTN_HEADS = 8
ATTN_HEAD_DIM = 64
ATTN_V_HEAD = 128
ATTN_QK_DIM = 1024
ATTN_V_DIM = 1024
REL_BUCKETS = 32
REL_MAX_DIST = 128

V7X_VMEM_BYTES = 64 * 1024 * 1024
VMEM_LIMIT = V7X_VMEM_BYTES - 8 * 1024 * 1024
LANES = 128
BF16_SUBLANES = 16

ROW_TILE = 512
HALO = BF16_SUBLANES
CONV_COLS = 512
ATTN_TILE = 1024
ATTN_KTILE = 1024
ATTN_ROWS_B = 64
ATTN_ONES_ROWS = BF16_SUBLANES
LOG2E = 1.4426950408889634
GS = SSD_HPG * SSD_HEADDIM
SCAN_DEPTH = 2


def _resident(shape):
    nd = len(shape)
    return pl.BlockSpec(shape, lambda *_: (0,) * nd, pipeline_mode=pl.Buffered(1))


def _silu(x):
    return x * (1.0 / (1.0 + jnp.exp(-x)))


def _softplus(x):
    return jnp.maximum(x, 0.0) + jnp.log(1.0 + jnp.exp(-jnp.abs(x)))


def _rms_rows(x, w_row):
    ms = jnp.mean(x * x, axis=-1, keepdims=True)
    return x * lax.rsqrt(ms + RMS_EPS) * w_row


def _dot(a, b):
    return jnp.dot(a, b, preferred_element_type=F32)


def _dot_nt(a, b):
    return lax.dot_general(a, b, (((1,), (1,)), ((), ())), preferred_element_type=F32)


def _split3(x):
    hi = x.astype(BF16)
    r1 = x - hi.astype(F32)
    mid = r1.astype(BF16)
    lo = (r1 - mid.astype(F32)).astype(BF16)
    return hi, mid, lo


def _ssd_in_kernel(h_ref, hp_ref, hn_ref, nw_ref, wz_ref, wxbc_ref, wdc_ref, wdr_ref,
                   cw_ref, cb_ref, biasc_ref, alogc_ref, biasr_ref, alogr_ref,
                   z_ref, x_ref, b_ref, c_ref, csc_ref, dtr_ref, csr_ref, pre_ref, *, seq):
    i = pl.program_id(0)
    tm = h_ref.shape[0]
    q = SSD_CHUNK
    nw = nw_ref[...]
    at_start = (i * tm) % seq == 0
    at_end = ((i + 1) * tm) % seq == 0
    xn = _rms_rows(h_ref[...], nw).astype(BF16)
    xp = (_rms_rows(hp_ref[...], nw) * jnp.where(at_start, 0.0, 1.0)).astype(BF16)
    xq = (_rms_rows(hn_ref[...], nw) * jnp.where(at_end, 0.0, 1.0)).astype(BF16)
    ext = jnp.concatenate([xp, xn, xq], axis=0)

    z_ref[...] = _dot(xn, wz_ref[...]).astype(BF16)

    rows = lax.broadcasted_iota(jnp.int32, (q, q), 0)
    cols = lax.broadcasted_iota(jnp.int32, (q, q), 1)
    lower_b = (cols <= rows).astype(BF16)
    upper_f = (rows <= cols).astype(F32)
    lower_f = (rows >= cols).astype(F32)
    a_cols = _softplus(_dot(xn, wdc_ref[...]) + biasc_ref[...]) * -jnp.exp(alogc_ref[...])
    bwd_lane = (lax.broadcasted_iota(jnp.int32, (q, LANES), 1) & (SSD_HEADS)) != 0
    dt_rows = _softplus(_dot_nt(wdr_ref[...], xn) + biasr_ref[...])
    a_rows = dt_rows * -jnp.exp(alogr_ref[...])
    dtr_ref[...] = dt_rows
    bwd_row = (lax.broadcasted_iota(jnp.int32, (2 * SSD_HEADS, q), 0) & SSD_HPG) != 0
    for j in range(tm // q):
        blk = a_cols[j * q:(j + 1) * q, :]
        hi, mid, lo = _split3(blk)
        r3 = _dot(lower_b, jnp.concatenate([hi, mid, lo], axis=1))
        pre = r3[:, :LANES] + r3[:, LANES:2 * LANES] + r3[:, 2 * LANES:]
        suf = pre[q - 1:q, :] - pre + blk
        csc_ref[j * q:(j + 1) * q, :] = jnp.where(bwd_lane, suf, pre)
        rblk = a_rows[:, j * q:(j + 1) * q]
        pre_r = jnp.dot(rblk, upper_f, preferred_element_type=F32, precision=lax.Precision.HIGHEST)
        suf_r = jnp.dot(rblk, lower_f, preferred_element_type=F32, precision=lax.Precision.HIGHEST)
        csr_ref[:, j * q:(j + 1) * q] = jnp.where(bwd_row, suf_r, pre_r)

    half = SSD_CONV // 2
    outs = ((x_ref, 0, SSD_D_INNER), (b_ref, SSD_D_INNER, SSD_GN),
            (c_ref, SSD_D_INNER + SSD_GN, SSD_GN))
    blocks = [(o_ref, base + j * CONV_COLS, j * CONV_COLS)
              for o_ref, base, width in outs for j in range(width // CONV_COLS)]

    def project(n):
        lo = blocks[n][1]
        pre_ref[n % 2] = _dot(ext, wxbc_ref[:, lo:lo + CONV_COLS])

    project(0)
    for n, (o_ref, lo, dst) in enumerate(blocks):
        if n + 1 < len(blocks):
            project(n + 1)
        w = cw_ref[:, lo:lo + CONV_COLS]
        acc = cb_ref[:, lo:lo + CONV_COLS]
        for k in range(SSD_CONV):
            off = HALO - half + k
            acc = acc + pre_ref[n % 2, off:off + tm, :] * w[k:k + 1, :]
        o_ref[:, dst:dst + CONV_COLS] = _silu(acc).astype(BF16)


def _ssd_in_proj(h2, seq, nw, wz, wxbc, wdc, wdr, conv_w, conv_b, biasc, alogc, biasr, alogr):
    t = h2.shape[0]
    tm = min(ROW_TILE, seq)
    assert seq % tm == 0 and tm % HALO == 0 and tm % SSD_CHUNK == 0 and biasr.shape[1] == tm
    per = tm // HALO
    last = t // HALO - 1
    row = lambda n: pl.BlockSpec((tm, n), lambda i: (i, 0))
    return pl.pallas_call(
        functools.partial(_ssd_in_kernel, seq=seq),
        name="ssd_in_proj",
        grid=(t // tm,),
        in_specs=[row(D_MODEL),
                  pl.BlockSpec((HALO, D_MODEL), lambda i: (jnp.maximum(i * per - 1, 0), 0)),
                  pl.BlockSpec((HALO, D_MODEL), lambda i: (jnp.minimum((i + 1) * per, last), 0)),
                  _resident(nw.shape), _resident(wz.shape), _resident(wxbc.shape),
                  _resident(wdc.shape), _resident(wdr.shape), _resident(conv_w.shape),
                  _resident(conv_b.shape), _resident(biasc.shape), _resident(alogc.shape),
                  _resident(biasr.shape), _resident(alogr.shape)],
        out_specs=[row(SSD_D_INNER), row(SSD_D_INNER), row(SSD_GN), row(SSD_GN), row(LANES),
                   pl.BlockSpec((2 * SSD_HEADS, tm), lambda i: (0, i)),
                   pl.BlockSpec((2 * SSD_HEADS, tm), lambda i: (0, i))],
        out_shape=[jax.ShapeDtypeStruct((t, SSD_D_INNER), BF16),
                   jax.ShapeDtypeStruct((t, SSD_D_INNER), BF16),
                   jax.ShapeDtypeStruct((t, SSD_GN), BF16),
                   jax.ShapeDtypeStruct((t, SSD_GN), BF16),
                   jax.ShapeDtypeStruct((t, LANES), F32),
                   jax.ShapeDtypeStruct((2 * SSD_HEADS, t), F32),
                   jax.ShapeDtypeStruct((2 * SSD_HEADS, t), F32)],
        scratch_shapes=[pltpu.VMEM((2, tm + 2 * HALO, CONV_COLS), F32)],
        compiler_params=pltpu.CompilerParams(dimension_semantics=("arbitrary",),
                                             vmem_limit_bytes=VMEM_LIMIT),
    )(h2, h2, h2, nw, wz, wxbc, wdc, wdr, conv_w, conv_b, biasc, alogc, biasr, alogr)


def _ssd_scan_kernel(x_ref, b_ref, c_ref, csc_ref, dtr_ref, csr_ref, d_ref,
                     y_ref, acc_ref, stf_ref, stb_ref, e_ref, *, seq):
    g = pl.program_id(1)
    q = SSD_CHUNK
    nchunk = seq // q

    rows = lax.broadcasted_iota(jnp.int32, (q, q), 0)
    cols = lax.broadcasted_iota(jnp.int32, (q, q), 1)
    lane = lax.broadcasted_iota(jnp.int32, (q, LANES), 1)
    lane_gs = lax.broadcasted_iota(jnp.int32, (q, GS), 1)

    ek = lax.broadcasted_iota(jnp.int32, (2 * LANES, SSD_HPG * q), 0)
    en = lax.broadcasted_iota(jnp.int32, (2 * LANES, SSD_HPG * q), 1) >> 7
    for d in range(2):
        col = d * SSD_HEADS + g * SSD_HPG + en
        e_ref[d] = ((ek == col) | (ek == col + 64) | (ek == col + LANES)).astype(BF16)

    d_row = d_ref[...]

    def front(c, rev):
        s0 = pl.multiple_of(c * q, q)
        xs_b = x_ref[0, pl.ds(s0, q), :]
        bm_b = b_ref[0, pl.ds(s0, q), :]
        cm_b = c_ref[0, pl.ds(s0, q), :]
        gmat = _dot_nt(cm_b, bm_b)
        bm_t = bm_b.astype(F32).T
        keep = (cols >= rows) if rev else (cols <= rows)
        grp_rows = pl.ds(pl.multiple_of(g * 8, 8), 8)
        return dict(xs_b=xs_b, cm_b=cm_b, gmat=gmat, bm_t=bm_t, keep=keep,
                    dt_rows=dtr_ref[grp_rows, pl.ds(s0, q)],
                    cs_rows=csr_ref[grp_rows, pl.ds(s0, q)],
                    cs_cols=csc_ref[0, pl.ds(s0, q), :])

    def bcast_lhs(f):
        hi, mid, lo = _split3(f["cs_cols"])
        return jnp.concatenate([jnp.where(lane < 64, hi, mid), lo], axis=1)

    def middle(f, bc, rev):
        d = 1 if rev else 0
        csx = jnp.concatenate(
            [jnp.where(lane < 64, bc[:, 0:q], bc[:, q:2 * q]),
             jnp.where(lane < 64, bc[:, 2 * q:3 * q], bc[:, 3 * q:4 * q])], axis=1)
        mmats, bts, xms = [], [], []
        for r in range(SSD_HPG):
            row = d * SSD_HPG + r
            cs_r = f["cs_rows"][row:row + 1, :]
            dt_r = f["dt_rows"][row:row + 1, :]
            end_r = cs_r[:, 0:1] if rev else cs_r[:, q - 1:q]
            seg = bc[:, r * q:(r + 1) * q] - cs_r
            lmat = jnp.exp(jnp.minimum(seg, 0.0))
            mmats.append(jnp.where(f["keep"], f["gmat"] * lmat * dt_r, 0.0).astype(BF16))
            xms.append(jnp.where((lane_gs >> 6) == r, f["xs_b"], jnp.zeros_like(f["xs_b"])))
            w_r = jnp.exp(end_r - cs_r) * dt_r
            bts.append((f["bm_t"] * w_r).astype(BF16))
        pairs = [(jnp.concatenate([jnp.concatenate(mmats[r:r + 2], axis=1),
                                   jnp.concatenate(bts[r:r + 2], axis=1)], axis=0),
                  jnp.concatenate(xms[r:r + 2], axis=0)) for r in range(0, SSD_HPG, 2)]
        return dict(cm_b=f["cm_b"], xs_b=f["xs_b"], decay=jnp.exp(csx),
                    end_decay=jnp.exp(csx[0:1, :] if rev else csx[q - 1:q, :]), pairs=pairs)

    def back(m, rev):
        st_ref = stb_ref if rev else stf_ref
        state = st_ref[...]
        y = _dot(m["cm_b"], state.astype(BF16)) * m["decay"]
        dstate = jnp.zeros((SSD_D_STATE, GS), F32)
        for lhs, xpair in m["pairs"]:
            both = _dot(lhs, xpair)
            y = y + both[0:q, :]
            dstate = dstate + both[q:, :]
        st_ref[...] = state * m["end_decay"] + dstate
        return y

    depth = SCAN_DEPTH

    def step(i, finish):
        work = ([(depth * i + k, False) for k in range(depth)]
                + [(nchunk - 1 - depth * i - k, True) for k in range(depth)])
        fronts = [front(c, rev) for c, rev in work]
        bcs = []
        for d in range(2):
            mine = fronts[d * depth:(d + 1) * depth]
            both = _dot(jnp.concatenate([bcast_lhs(f) for f in mine], axis=0), e_ref[d])
            bcs += [both[k * q:(k + 1) * q, :] for k in range(depth)]
        mids = [middle(f, bc, rev) for f, bc, (_, rev) in zip(fronts, bcs, work)]
        for k in [j + d * depth for j in range(depth) for d in range(2)]:
            m, (c, rev) = mids[k], work[k]
            y = back(m, rev)
            if not rev:
                y = y + m["xs_b"].astype(F32) * d_row
            rs = pl.ds(pl.multiple_of(c * q, q), q)
            if finish:
                y_ref[0, rs, :] = (acc_ref[rs, :] + y).astype(y_ref.dtype)
            else:
                acc_ref[rs, :] = y

    stf_ref[...] = jnp.zeros_like(stf_ref)
    stb_ref[...] = jnp.zeros_like(stb_ref)

    def first_half(i, carry):
        step(i, False)
        return carry

    def second_half(i, carry):
        step(i, True)
        return carry

    steps = nchunk // depth
    lax.fori_loop(0, steps // 2, first_half, 0)
    lax.fori_loop(steps // 2, steps, second_half, 0)


def _ssd_scan(x3, b3, c3, csc3, dtr, csr, d_exp):
    bsz, seq, _ = x3.shape
    assert seq % (2 * SCAN_DEPTH * SSD_CHUNK) == 0
    grp = lambda n: pl.BlockSpec((1, seq, n), lambda b, g: (b, 0, g))
    heads_rows = pl.BlockSpec((2 * SSD_HEADS, seq), lambda b, g: (0, b))
    return pl.pallas_call(
        functools.partial(_ssd_scan_kernel, seq=seq),
        name="ssd_scan",
        grid=(bsz, SSD_GROUPS),
        in_specs=[grp(GS), grp(SSD_D_STATE), grp(SSD_D_STATE),
                  pl.BlockSpec((1, seq, LANES), lambda b, g: (b, 0, 0)),
                  heads_rows, heads_rows,
                  pl.BlockSpec((1, GS), lambda b, g: (0, g))],
        out_specs=pl.BlockSpec((1, seq, GS), lambda b, g: (b, 0, g)),
        out_shape=jax.ShapeDtypeStruct((bsz, seq, SSD_D_INNER), BF16),
        scratch_shapes=[pltpu.VMEM((seq, GS), F32),
                        pltpu.VMEM((SSD_D_STATE, GS), F32),
                        pltpu.VMEM((SSD_D_STATE, GS), F32),
                        pltpu.VMEM((2, 2 * LANES, SSD_HPG * SSD_CHUNK), BF16)],
        compiler_params=pltpu.CompilerParams(dimension_semantics=("arbitrary", "arbitrary"),
                                             vmem_limit_bytes=VMEM_LIMIT),
    )(x3, b3, c3, csc3, dtr, csr, d_exp)


def _attn_in_body(h, nw_ref, wqt_ref, wk_ref, wvt_ref, wz_ref, qn_ref, kn_ref, ones_ref,
                  qt_ref, k_ref, vt_ref, z_ref):
    tm = h.shape[0]
    xn = _rms_rows(h, nw_ref[...]).astype(BF16)
    inv_d = 1.0 / ATTN_HEAD_DIM

    yq = _dot_nt(wqt_ref[...], xn).reshape(-1, ATTN_HEAD_DIM, tm)
    ssq = jnp.sum(yq * yq, axis=1, keepdims=True)
    qn = (yq * lax.rsqrt(ssq * inv_d + RMS_EPS)).reshape(ATTN_QK_DIM, tm)
    qt_ref[...] = (qn * qn_ref[...]).astype(BF16)

    yk = _dot(xn, wk_ref[...])
    ssk = _dot((yk * yk).astype(BF16), ones_ref[...])
    k_ref[...] = (yk * lax.rsqrt(ssk * inv_d + RMS_EPS) * kn_ref[...]).astype(BF16)

    vt_ref[...] = _dot_nt(wvt_ref[...], xn).astype(BF16)
    z_ref[...] = _dot(xn, wz_ref[...]).astype(BF16)


def _ssd_out_attn_in_kernel(y_ref, zs_ref, snw_ref, wo_ref, h_ref, *rest):
    u = y_ref[...].astype(F32) * _silu(zs_ref[...].astype(F32))
    un = _rms_rows(u, snw_ref[...]).astype(BF16)
    h_new = h_ref[...] + _dot(un, wo_ref[...])
    attn_in_refs, (hn_ref, qt_ref, k_ref, vt_ref, z_ref) = rest[:-5], rest[-5:]
    hn_ref[...] = h_new
    _attn_in_body(h_new, *attn_in_refs, qt_ref, k_ref, vt_ref, z_ref)


def _ssd_out_attn_in(y2, zs2, snw, wo, h2, nw, wqt, wk, wvt, wz, qn_cols, kn, ones_bd):
    t = h2.shape[0]
    tm = qn_cols.shape[1]
    row = lambda n: pl.BlockSpec((tm, n), lambda i: (i, 0))
    col = lambda n: pl.BlockSpec((n, tm), lambda i: (0, i))
    attn_consts = (nw, wqt, wk, wvt, wz, qn_cols, kn, ones_bd)
    return pl.pallas_call(
        _ssd_out_attn_in_kernel,
        name="ssd_out_attn_in",
        grid=(t // tm,),
        in_specs=[row(SSD_D_INNER), row(SSD_D_INNER), _resident(snw.shape), _resident(wo.shape),
                  row(D_MODEL)] + [_resident(a.shape) for a in attn_consts],
        out_specs=[row(D_MODEL), col(ATTN_QK_DIM), row(ATTN_QK_DIM), col(ATTN_V_DIM),
                   row(ATTN_V_DIM)],
        out_shape=[jax.ShapeDtypeStruct((t, D_MODEL), F32),
                   jax.ShapeDtypeStruct((ATTN_QK_DIM, t), BF16),
                   jax.ShapeDtypeStruct((t, ATTN_QK_DIM), BF16),
                   jax.ShapeDtypeStruct((ATTN_V_DIM, t), BF16),
                   jax.ShapeDtypeStruct((t, ATTN_V_DIM), BF16)],
        compiler_params=pltpu.CompilerParams(dimension_semantics=("arbitrary",),
                                             vmem_limit_bytes=VMEM_LIMIT),
    )(y2, zs2, snw, wo, h2, *attn_consts)


def _attn_kernel(thr_ref, rb_ref, qt_ref, k_ref, vt_ref, z_ref, lam_ref, subln_ref, o_ref,
                 bias_ref, sa_ref, sb_ref, p_ref, m_ref, acc_ref, vte_ref, *, seq, tile, ktile,
                 lam_init):
    h = pl.program_id(0)
    b = pl.program_id(1)
    qi = pl.program_id(2)
    nk = seq // ktile
    kfac = ktile // tile
    half_buckets = REL_BUCKETS // 2
    sub = 8

    @pl.when((b == 0) & (qi == 0))
    def _build_bias():
        rel0 = (lax.broadcasted_iota(jnp.int32, (ktile, tile), 0)
                - lax.broadcasted_iota(jnp.int32, (ktile, tile), 1))
        for idx in range(kfac + 2):
            rel = rel0 + (idx - kfac) * tile
            n = jnp.abs(rel)
            neg = jnp.full((ktile, tile), rb_ref[h], F32)
            pos = jnp.full((ktile, tile), rb_ref[half_buckets * ATTN_HEADS + h], F32)
            for t in range(1, half_buckets):
                ge = n >= thr_ref[t]
                neg = jnp.where(ge, rb_ref[t * ATTN_HEADS + h], neg)
                pos = jnp.where(ge, rb_ref[(half_buckets + t) * ATTN_HEADS + h], pos)
            bias_ref[idx] = jnp.where(rel > 0, pos, neg) * LOG2E

    @pl.when(qi == 0)
    def _extend_v():
        vte_ref[0:ATTN_V_HEAD, :] = vt_ref[...]
        vte_ref[ATTN_V_HEAD:, :] = jnp.ones((ATTN_ONES_ROWS, seq), BF16)

    c_left = rb_ref[(half_buckets - 1) * ATTN_HEADS + h] * LOG2E
    c_right = rb_ref[(2 * half_buckets - 1) * ATTN_HEADS + h] * LOG2E

    def offset(kc):
        return kfac * kc - qi

    def is_near(kc):
        return (offset(kc) >= -kfac) & (offset(kc) <= 1)

    def far_bias(kc):
        return jnp.where(is_near(kc), 0.0, jnp.where(offset(kc) < 0, c_left, c_right))

    q_t = qt_ref[...]
    row = lax.broadcasted_iota(jnp.int32, (ATTN_V_HEAD, tile), 0)
    zero = jnp.zeros_like(q_t)
    qs = (jnp.where(row < ATTN_HEAD_DIM, q_t, zero), jnp.where(row >= ATTN_HEAD_DIM, q_t, zero))

    m_ref[...] = jnp.full(m_ref.shape, -jnp.inf, F32)
    acc_ref[...] = jnp.zeros(acc_ref.shape, F32)

    def scores(kc, t, s_out, with_bias):
        kt = k_ref[0, pl.ds(pl.multiple_of(kc * ktile, ktile), ktile), :]
        sb = _dot(kt, qs[t])
        if with_bias:
            sb = sb + bias_ref[jnp.clip(offset(kc) + kfac, 0, kfac + 1)]
        s_out[t] = sb
        m8 = sb[0:sub, :]
        for i in range(1, ktile // sub):
            m8 = jnp.maximum(m8, sb[i * sub:(i + 1) * sub, :])
        return m8

    def step_body(kc_next, s_next, kc, s_cur, m8s, with_bias):
        vt = vte_ref[:, pl.ds(pl.multiple_of(kc * ktile, ktile), ktile)]
        shift = far_bias(kc)
        m8s_next = []
        for t in range(2):
            m_old = m_ref[t]
            m_new = jnp.maximum(m_old, jnp.max(m8s[t], axis=0, keepdims=True) + shift)
            alpha = jnp.exp2(m_old - m_new)
            msub = m_new - shift
            if kc_next is not None:
                m8s_next.append(scores(kc_next, t, s_next, with_bias))
            for rb in range(ktile // ATTN_ROWS_B):
                rows = pl.ds(rb * ATTN_ROWS_B, ATTN_ROWS_B)
                p_ref[t, rows, :] = jnp.exp2(s_cur[t, rows, :] - msub).astype(BF16)
            acc_ref[t] = alpha * acc_ref[t] + _dot(vt, p_ref[t])
            m_ref[t] = m_new
        return tuple(m8s_next)

    def step(kc_next, s_next, kc, s_cur, m8s):
        return lax.cond(is_near(kc_next),
                        lambda: step_body(kc_next, s_next, kc, s_cur, m8s, True),
                        lambda: step_body(kc_next, s_next, kc, s_cur, m8s, False))

    def chunk_pair(j, m8s):
        c0 = 2 * j
        m8s_odd = step(c0 + 1, sb_ref, c0, sa_ref, m8s)
        return step(c0 + 2, sa_ref, c0 + 1, sb_ref, m8s_odd)

    def first_scores(with_bias):
        def fn():
            return tuple(scores(0, t, sa_ref, with_bias) for t in range(2))
        return fn

    m8s = lax.fori_loop(0, nk // 2 - 1, chunk_pair,
                        lax.cond(is_near(0), first_scores(True), first_scores(False)))
    m8s = step(nk - 1, sb_ref, nk - 2, sa_ref, m8s)
    step_body(None, None, nk - 1, sb_ref, m8s, False)

    lv = lam_ref[...]
    lam = (jnp.exp(jnp.sum(lv[0:1] * lv[1:2], axis=-1, keepdims=True))
           - jnp.exp(jnp.sum(lv[2:3] * lv[3:4], axis=-1, keepdims=True)) + lam_init)
    o1 = acc_ref[0, 0:ATTN_V_HEAD, :] / acc_ref[0, ATTN_V_HEAD:ATTN_V_HEAD + 1, :]
    o2 = acc_ref[1, 0:ATTN_V_HEAD, :] / acc_ref[1, ATTN_V_HEAD:ATTN_V_HEAD + 1, :]
    o = (o1 - lam * o2).T
    o = _rms_rows(o, subln_ref[...]) * (1.0 - lam_init)
    o_ref[0] = (o * _silu(z_ref[0].astype(F32))).astype(o_ref.dtype)


def _attention(thr, rb_flat, qt, k3, vt, z3, lam_vec, subln, lam_init):
    bsz, seq, _ = k3.shape
    tile = min(ATTN_TILE, seq)
    ktile = min(ATTN_KTILE, seq // 2)
    nq = seq // tile
    assert tile > REL_MAX_DIST and ktile % tile == 0 and seq % (2 * ktile) == 0
    assert ktile % ATTN_ROWS_B == 0
    smem = pl.BlockSpec(memory_space=pltpu.SMEM)
    qspec = pl.BlockSpec((ATTN_V_HEAD, tile), lambda h, b, i: (h, b * nq + i))
    kspec = pl.BlockSpec((1, seq, ATTN_V_HEAD), lambda h, b, i: (b, 0, h))
    vspec = pl.BlockSpec((ATTN_V_HEAD, seq), lambda h, b, i: (h, b))
    zspec = pl.BlockSpec((1, tile, ATTN_V_HEAD), lambda h, b, i: (b, i, h))
    return pl.pallas_call(
        functools.partial(_attn_kernel, seq=seq, tile=tile, ktile=ktile, lam_init=lam_init),
        name="diff_attention",
        grid=(ATTN_HEADS, bsz, nq),
        in_specs=[smem, smem, qspec, kspec, vspec, zspec,
                  pl.BlockSpec(lam_vec.shape, lambda h, b, i: (0, 0)),
                  pl.BlockSpec(subln.shape, lambda h, b, i: (0, 0))],
        out_specs=zspec,
        out_shape=jax.ShapeDtypeStruct((bsz, seq, ATTN_V_DIM), BF16),
        scratch_shapes=[pltpu.VMEM((ktile // tile + 2, ktile, tile), F32),
                        pltpu.VMEM((2, ktile, tile), F32),
                        pltpu.VMEM((2, ktile, tile), F32),
                        pltpu.VMEM((2, ktile, tile), BF16),
                        pltpu.VMEM((2, 1, tile), F32),
                        pltpu.VMEM((2, ATTN_V_HEAD + ATTN_ONES_ROWS, tile), F32),
                        pltpu.VMEM((ATTN_V_HEAD + ATTN_ONES_ROWS, seq), BF16)],
        compiler_params=pltpu.CompilerParams(
            dimension_semantics=("arbitrary", "arbitrary", "arbitrary"),
            vmem_limit_bytes=VMEM_LIMIT),
    )(thr, rb_flat, qt, k3, vt, z3, lam_vec, subln)


def _attn_out_kernel(o_ref, w_ref, h_ref, out_ref):
    out_ref[...] = h_ref[...] + _dot(o_ref[...], w_ref[...])


def _attn_out_proj(o2, w, h2):
    t = h2.shape[0]
    tm = min(ROW_TILE, t)
    row = lambda n: pl.BlockSpec((tm, n), lambda i: (i, 0))
    return pl.pallas_call(
        _attn_out_kernel,
        name="attn_out_proj",
        grid=(t // tm,),
        in_specs=[row(ATTN_V_DIM), _resident(w.shape), row(D_MODEL)],
        out_specs=row(D_MODEL),
        out_shape=jax.ShapeDtypeStruct((t, D_MODEL), F32),
        compiler_params=pltpu.CompilerParams(dimension_semantics=("arbitrary",),
                                             vmem_limit_bytes=VMEM_LIMIT),
    )(o2, w, h2)


def _ssd_mixer(h2, bsz, seq, nw, w_in, conv_w, conv_b, dt_bias, a_log, d_skip):
    wb16 = w_in.astype(BF16)
    o_x = SSD_D_INNER
    o_b = 2 * SSD_D_INNER
    o_c = o_b + SSD_GN
    o_dt = o_c + SSD_GN
    wdt = wb16[:, o_dt:]
    wdc = jnp.concatenate([wdt, wdt], axis=1)
    perm = lambda a: a.reshape(2, SSD_GROUPS, SSD_HPG).transpose(1, 0, 2).reshape(2 * SSD_HEADS)
    wdr = wdt.T.reshape(2, SSD_GROUPS, SSD_HPG, D_MODEL).transpose(1, 0, 2, 3).reshape(
        2 * SSD_HEADS, D_MODEL)
    del o_b, o_c
    tm = min(ROW_TILE, seq)
    flat = lambda a: a.reshape(2 * SSD_HEADS).astype(F32)
    biasc = jnp.tile(flat(dt_bias), 2).reshape(1, LANES)
    alogc = jnp.tile(flat(a_log), 2).reshape(1, LANES)
    biasr = jnp.broadcast_to(perm(flat(dt_bias))[:, None], (2 * SSD_HEADS, tm))
    alogr = jnp.broadcast_to(perm(flat(a_log))[:, None], (2 * SSD_HEADS, tm))
    d_exp = jnp.repeat(d_skip.astype(F32), SSD_HEADDIM).reshape(1, SSD_D_INNER)
    z2, x2, b2, c2, csc, dtr, csr = _ssd_in_proj(
        h2, seq, nw.reshape(1, -1), wb16[:, :o_x], wb16[:, o_x:o_dt], wdc, wdr,
        conv_w.astype(F32), conv_b.reshape(1, -1).astype(F32), biasc, alogc, biasr, alogr)

    y3 = _ssd_scan(x2.reshape(bsz, seq, -1), b2.reshape(bsz, seq, -1), c2.reshape(bsz, seq, -1),
                   csc.reshape(bsz, seq, LANES), dtr, csr, d_exp)
    return y3.reshape(bsz * seq, -1), z2


def _t5_thresholds():
    nb = REL_BUCKETS // 2
    max_exact = nb // 2
    n = jnp.arange(REL_MAX_DIST + 1, dtype=jnp.int32)
    nf = jnp.maximum(n, 1).astype(F32)
    large = max_exact + (jnp.log(nf / max_exact) / math.log(REL_MAX_DIST / max_exact)
                         * (nb - max_exact)).astype(jnp.int32)
    bucket = jnp.where(n < max_exact, n, jnp.minimum(large, nb - 1))
    t = jnp.arange(nb, dtype=jnp.int32)
    return jnp.sum((bucket[None, :] < t[:, None]).astype(jnp.int32), axis=1)


def _attn_layer(y2, zs2, ssd_norm_w, ssd_w_out, h2, bsz, seq, nw, w_in, q_norm, k_norm, lam_vec,
                subln, w_out, thr, rb_flat, ones_bd, layer_idx):
    wb16 = w_in.astype(BF16)
    reps = ATTN_QK_DIM // ATTN_HEAD_DIM
    tm = min(ROW_TILE, bsz * seq)
    qn = jnp.tile(q_norm.astype(F32), reps) * (ATTN_HEAD_DIM ** -0.5 * LOG2E)
    qn_cols = jnp.broadcast_to(qn[:, None], (ATTN_QK_DIM, tm))
    kn = jnp.tile(k_norm.astype(F32), reps).reshape(1, -1)
    h2, qt, k2, vt, z2 = _ssd_out_attn_in(
        y2, zs2, ssd_norm_w.reshape(1, -1).astype(F32), ssd_w_out.astype(BF16), h2,
        nw.reshape(1, -1), wb16[:, :ATTN_QK_DIM].T, wb16[:, ATTN_QK_DIM:2 * ATTN_QK_DIM],
        wb16[:, 2 * ATTN_QK_DIM:2 * ATTN_QK_DIM + ATTN_V_DIM].T,
        wb16[:, 2 * ATTN_QK_DIM + ATTN_V_DIM:], qn_cols, kn, ones_bd)
    lam_init = 0.8 - 0.6 * math.exp(-0.3 * layer_idx)
    r3 = lambda a: a.reshape(bsz, seq, -1)
    o3 = _attention(thr, rb_flat, qt, r3(k2), vt, r3(z2), lam_vec.astype(F32),
                    subln.reshape(1, -1).astype(F32), lam_init)
    return _attn_out_proj(o3.reshape(bsz * seq, -1), w_out.astype(BF16), h2)


def kernel(x, norm_w, ssd_w_in, ssd_conv_w, ssd_conv_b, ssd_dt_bias, ssd_a_log, ssd_d, ssd_norm_w, ssd_w_out, attn_w_in, attn_q_norm, attn_k_norm, attn_lambda, attn_subln, attn_w_out, rel_bias):
    bsz, seq, _ = x.shape
    depth = norm_w.shape[0]
    h2 = x.reshape(bsz * seq, D_MODEL)
    thr = _t5_thresholds()
    rb_flat = rel_bias.astype(F32).reshape(-1)
    blk = jnp.arange(ATTN_QK_DIM, dtype=jnp.int32) // ATTN_HEAD_DIM
    ones_bd = (blk[:, None] == blk[None, :]).astype(BF16)
    assert depth % 2 == 0
    for j in range(depth // 2):
        y2, zs2 = _ssd_mixer(h2, bsz, seq, norm_w[2 * j], ssd_w_in[j], ssd_conv_w[j], ssd_conv_b[j],
                             ssd_dt_bias[j], ssd_a_log[j], ssd_d[j])
        h2 = _attn_layer(y2, zs2, ssd_norm_w[j], ssd_w_out[j], h2, bsz, seq, norm_w[2 * j + 1],
                         attn_w_in[j], attn_q_norm[j], attn_k_norm[j], attn_lambda[j],
                         attn_subln[j], attn_w_out[j], thr, rb_flat, ones_bd, 2 * j + 1)
    return h2.reshape(bsz, seq, D_MODEL)
```

```python
import functools
import math

import jax
import jax.numpy as jnp
from jax import lax
from jax.experimental import pallas as pl
from jax.experimental.pallas import tpu as pltpu

F32 = jnp.float32
BF16 = jnp.bfloat16

RMS_EPS = 1e-6

D_MODEL = 1024
SSD_D_INNER = 2048
SSD_HEADDIM = 64
SSD_HEADS = 32
SSD_D_STATE = 128
SSD_GROUPS = 8
SSD_HPG = 4
SSD_CONV = 5
SSD_CHUNK = 128
SSD_GN = SSD_GROUPS * SSD_D_STATE
ATTN_HEADS = 8
ATTN_HEAD_DIM = 64
ATTN_V_HEAD = 128
ATTN_QK_DIM = 1024
ATTN_V_DIM = 1024
REL_BUCKETS = 32
REL_MAX_DIST = 128

V7X_VMEM_BYTES = 64 * 1024 * 1024
VMEM_LIMIT = V7X_VMEM_BYTES - 8 * 1024 * 1024
LANES = 128
BF16_SUBLANES = 16

ROW_TILE = 512
HALO = BF16_SUBLANES
CONV_COLS = 512
ATTN_TILE = 1024
ATTN_KTILE = 1024
ATTN_ROWS_B = 64
ATTN_ONES_ROWS = BF16_SUBLANES
LOG2E = 1.4426950408889634
GS = SSD_HPG * SSD_HEADDIM
SCAN_DEPTH = 2


def _resident(shape):
    nd = len(shape)
    return pl.BlockSpec(shape, lambda *_: (0,) * nd, pipeline_mode=pl.Buffered(1))


def _silu(x):
    return x * (1.0 / (1.0 + jnp.exp(-x)))


def _softplus(x):
    return jnp.maximum(x, 0.0) + jnp.log(1.0 + jnp.exp(-jnp.abs(x)))


def _rms_rows(x, w_row):
    ms = jnp.mean(x * x, axis=-1, keepdims=True)
    return x * lax.rsqrt(ms + RMS_EPS) * w_row


def _dot(a, b):
    return jnp.dot(a, b, preferred_element_type=F32)


def _dot_nt(a, b):
    return lax.dot_general(a, b, (((1,), (1,)), ((), ())), preferred_element_type=F32)


def _split3(x):
    hi = x.astype(BF16)
    r1 = x - hi.astype(F32)
    mid = r1.astype(BF16)
    lo = (r1 - mid.astype(F32)).astype(BF16)
    return hi, mid, lo


def _ssd_in_kernel(h_ref, hp_ref, hn_ref, nw_ref, wz_ref, wxbc_ref, wdc_ref, wdr_ref,
                   cw_ref, cb_ref, biasc_ref, alogc_ref, biasr_ref, alogr_ref,
                   z_ref, x_ref, b_ref, c_ref, csc_ref, dtr_ref, csr_ref, pre_ref, *, seq):
    i = pl.program_id(0)
    tm = h_ref.shape[0]
    q = SSD_CHUNK
    nw = nw_ref[...]
    at_start = (i * tm) % seq == 0
    at_end = ((i + 1) * tm) % seq == 0
    xn = _rms_rows(h_ref[...], nw).astype(BF16)
    xp = (_rms_rows(hp_ref[...], nw) * jnp.where(at_start, 0.0, 1.0)).astype(BF16)
    xq = (_rms_rows(hn_ref[...], nw) * jnp.where(at_end, 0.0, 1.0)).astype(BF16)
    ext = jnp.concatenate([xp, xn, xq], axis=0)

    z_ref[...] = _dot(xn, wz_ref[...]).astype(BF16)

    rows = lax.broadcasted_iota(jnp.int32, (q, q), 0)
    cols = lax.broadcasted_iota(jnp.int32, (q, q), 1)
    lower_b = (cols <= rows).astype(BF16)
    upper_f = (rows <= cols).astype(F32)
    lower_f = (rows >= cols).astype(F32)
    a_cols = _softplus(_dot(xn, wdc_ref[...]) + biasc_ref[...]) * -jnp.exp(alogc_ref[...])
    bwd_lane = (lax.broadcasted_iota(jnp.int32, (q, LANES), 1) & (SSD_HEADS)) != 0
    dt_rows = _softplus(_dot_nt(wdr_ref[...], xn) + biasr_ref[...])
    a_rows = dt_rows * -jnp.exp(alogr_ref[...])
    dtr_ref[...] = dt_rows
    bwd_row = (lax.broadcasted_iota(jnp.int32, (2 * SSD_HEADS, q), 0) & SSD_HPG) != 0
    for j in range(tm // q):
        blk = a_cols[j * q:(j + 1) * q, :]
        hi, mid, lo = _split3(blk)
        r3 = _dot(lower_b, jnp.concatenate([hi, mid, lo], axis=1))
        pre = r3[:, :LANES] + r3[:, LANES:2 * LANES] + r3[:, 2 * LANES:]
        suf = pre[q - 1:q, :] - pre + blk
        csc_ref[j * q:(j + 1) * q, :] = jnp.where(bwd_lane, suf, pre)
        rblk = a_rows[:, j * q:(j + 1) * q]
        pre_r = jnp.dot(rblk, upper_f, preferred_element_type=F32, precision=lax.Precision.HIGHEST)
        suf_r = jnp.dot(rblk, lower_f, preferred_element_type=F32, precision=lax.Precision.HIGHEST)
        csr_ref[:, j * q:(j + 1) * q] = jnp.where(bwd_row, suf_r, pre_r)

    half = SSD_CONV // 2
    outs = ((x_ref, 0, SSD_D_INNER), (b_ref, SSD_D_INNER, SSD_GN),
            (c_ref, SSD_D_INNER + SSD_GN, SSD_GN))
    blocks = [(o_ref, base + j * CONV_COLS, j * CONV_COLS)
              for o_ref, base, width in outs for j in range(width // CONV_COLS)]

    def project(n):
        lo = blocks[n][1]
        pre_ref[n % 2] = _dot(ext, wxbc_ref[:, lo:lo + CONV_COLS])

    project(0)
    for n, (o_ref, lo, dst) in enumerate(blocks):
        if n + 1 < len(blocks):
            project(n + 1)
        w = cw_ref[:, lo:lo + CONV_COLS]
        acc = cb_ref[:, lo:lo + CONV_COLS]
        for k in range(SSD_CONV):
            off = HALO - half + k
            acc = acc + pre_ref[n % 2, off:off + tm, :] * w[k:k + 1, :]
        o_ref[:, dst:dst + CONV_COLS] = _silu(acc).astype(BF16)


def _ssd_in_proj(h2, seq, nw, wz, wxbc, wdc, wdr, conv_w, conv_b, biasc, alogc, biasr, alogr):
    t = h2.shape[0]
    tm = min(ROW_TILE, seq)
    assert seq % tm == 0 and tm % HALO == 0 and tm % SSD_CHUNK == 0 and biasr.shape[1] == tm
    per = tm // HALO
    last = t // HALO - 1
    row = lambda n: pl.BlockSpec((tm, n), lambda i: (i, 0))
    return pl.pallas_call(
        functools.partial(_ssd_in_kernel, seq=seq),
        name="ssd_in_proj",
        grid=(t // tm,),
        in_specs=[row(D_MODEL),
                  pl.BlockSpec((HALO, D_MODEL), lambda i: (jnp.maximum(i * per - 1, 0), 0)),
                  pl.BlockSpec((HALO, D_MODEL), lambda i: (jnp.minimum((i + 1) * per, last), 0)),
                  _resident(nw.shape), _resident(wz.shape), _resident(wxbc.shape),
                  _resident(wdc.shape), _resident(wdr.shape), _resident(conv_w.shape),
                  _resident(conv_b.shape), _resident(biasc.shape), _resident(alogc.shape),
                  _resident(biasr.shape), _resident(alogr.shape)],
        out_specs=[row(SSD_D_INNER), row(SSD_D_INNER), row(SSD_GN), row(SSD_GN), row(LANES),
                   pl.BlockSpec((2 * SSD_HEADS, tm), lambda i: (0, i)),
                   pl.BlockSpec((2 * SSD_HEADS, tm), lambda i: (0, i))],
        out_shape=[jax.ShapeDtypeStruct((t, SSD_D_INNER), BF16),
                   jax.ShapeDtypeStruct((t, SSD_D_INNER), BF16),
                   jax.ShapeDtypeStruct((t, SSD_GN), BF16),
                   jax.ShapeDtypeStruct((t, SSD_GN), BF16),
                   jax.ShapeDtypeStruct((t, LANES), F32),
                   jax.ShapeDtypeStruct((2 * SSD_HEADS, t), F32),
                   jax.ShapeDtypeStruct((2 * SSD_HEADS, t), F32)],
        scratch_shapes=[pltpu.VMEM((2, tm + 2 * HALO, CONV_COLS), F32)],
        compiler_params=pltpu.CompilerParams(dimension_semantics=("arbitrary",),
                                             vmem_limit_bytes=VMEM_LIMIT),
    )(h2, h2, h2, nw, wz, wxbc, wdc, wdr, conv_w, conv_b, biasc, alogc, biasr, alogr)


def _ssd_scan_kernel(x_ref, b_ref, c_ref, csc_ref, dtr_ref, csr_ref, d_ref,
                     y_ref, acc_ref, stf_ref, stb_ref, e_ref, *, seq):
    g = pl.program_id(1)
    q = SSD_CHUNK
    nchunk = seq // q

    rows = lax.broadcasted_iota(jnp.int32, (q, q), 0)
    cols = lax.broadcasted_iota(jnp.int32, (q, q), 1)
    lane = lax.broadcasted_iota(jnp.int32, (q, LANES), 1)
    lane_gs = lax.broadcasted_iota(jnp.int32, (q, GS), 1)

    ek = lax.broadcasted_iota(jnp.int32, (2 * LANES, SSD_HPG * q), 0)
    en = lax.broadcasted_iota(jnp.int32, (2 * LANES, SSD_HPG * q), 1) >> 7
    for d in range(2):
        col = d * SSD_HEADS + g * SSD_HPG + en
        e_ref[d] = ((ek == col) | (ek == col + 64) | (ek == col + LANES)).astype(BF16)

    d_row = d_ref[...]

    def front(c, rev):
        s0 = pl.multiple_of(c * q, q)
        xs_b = x_ref[0, pl.ds(s0, q), :]
        bm_b = b_ref[0, pl.ds(s0, q), :]
        cm_b = c_ref[0, pl.ds(s0, q), :]
        gmat = _dot_nt(cm_b, bm_b)
        bm_t = bm_b.astype(F32).T
        keep = (cols >= rows) if rev else (cols <= rows)
        grp_rows = pl.ds(pl.multiple_of(g * 8, 8), 8)
        return dict(xs_b=xs_b, cm_b=cm_b, gmat=gmat, bm_t=bm_t, keep=keep,
                    dt_rows=dtr_ref[grp_rows, pl.ds(s0, q)],
                    cs_rows=csr_ref[grp_rows, pl.ds(s0, q)],
                    cs_cols=csc_ref[0, pl.ds(s0, q), :])

    def bcast_lhs(f):
        hi, mid, lo = _split3(f["cs_cols"])
        return jnp.concatenate([jnp.where(lane < 64, hi, mid), lo], axis=1)

    def middle(f, bc, rev):
        d = 1 if rev else 0
        csx = jnp.concatenate(
            [jnp.where(lane < 64, bc[:, 0:q], bc[:, q:2 * q]),
             jnp.where(lane < 64, bc[:, 2 * q:3 * q], bc[:, 3 * q:4 * q])], axis=1)
        mmats, bts, xms = [], [], []
        for r in range(SSD_HPG):
            row = d * SSD_HPG + r
            cs_r = f["cs_rows"][row:row + 1, :]
            dt_r = f["dt_rows"][row:row + 1, :]
            end_r = cs_r[:, 0:1] if rev else cs_r[:, q - 1:q]
            seg = bc[:, r * q:(r + 1) * q] - cs_r
            lmat = jnp.exp(jnp.minimum(seg, 0.0))
            mmats.append(jnp.where(f["keep"], f["gmat"] * lmat * dt_r, 0.0).astype(BF16))
            xms.append(jnp.where((lane_gs >> 6) == r, f["xs_b"], jnp.zeros_like(f["xs_b"])))
            w_r = jnp.exp(end_r - cs_r) * dt_r
            bts.append((f["bm_t"] * w_r).astype(BF16))
        pairs = [(jnp.concatenate([jnp.concatenate(mmats[r:r + 2], axis=1),
                                   jnp.concatenate(bts[r:r + 2], axis=1)], axis=0),
                  jnp.concatenate(xms[r:r + 2], axis=0)) for r in range(0, SSD_HPG, 2)]
        return dict(cm_b=f["cm_b"], xs_b=f["xs_b"], decay=jnp.exp(csx),
                    end_decay=jnp.exp(csx[0:1, :] if rev else csx[q - 1:q, :]), pairs=pairs)

    def back(m, rev):
        st_ref = stb_ref if rev else stf_ref
        state = st_ref[...]
        y = _dot(m["cm_b"], state.astype(BF16)) * m["decay"]
        dstate = jnp.zeros((SSD_D_STATE, GS), F32)
        for lhs, xpair in m["pairs"]:
            both = _dot(lhs, xpair)
            y = y + both[0:q, :]
            dstate = dstate + both[q:, :]
        st_ref[...] = state * m["end_decay"] + dstate
        return y

    depth = SCAN_DEPTH

    def step(i, finish):
        work = ([(depth * i + k, False) for k in range(depth)]
                + [(nchunk - 1 - depth * i - k, True) for k in range(depth)])
        fronts = [front(c, rev) for c, rev in work]
        bcs = []
        for d in range(2):
            mine = fronts[d * depth:(d + 1) * depth]
            both = _dot(jnp.concatenate([bcast_lhs(f) for f in mine], axis=0), e_ref[d])
            bcs += [both[k * q:(k + 1) * q, :] for k in range(depth)]
        mids = [middle(f, bc, rev) for f, bc, (_, rev) in zip(fronts, bcs, work)]
        for k in [j + d * depth for j in range(depth) for d in range(2)]:
            m, (c, rev) = mids[k], work[k]
            y = back(m, rev)
            if not rev:
                y = y + m["xs_b"].astype(F32) * d_row
            rs = pl.ds(pl.multiple_of(c * q, q), q)
            if finish:
                y_ref[0, rs, :] = (acc_ref[rs, :] + y).astype(y_ref.dtype)
            else:
                acc_ref[rs, :] = y

    stf_ref[...] = jnp.zeros_like(stf_ref)
    stb_ref[...] = jnp.zeros_like(stb_ref)

    def first_half(i, carry):
        step(i, False)
        return carry

    def second_half(i, carry):
        step(i, True)
        return carry

    steps = nchunk // depth
    lax.fori_loop(0, steps // 2, first_half, 0)
    lax.fori_loop(steps // 2, steps, second_half, 0)


def _ssd_scan(x3, b3, c3, csc3, dtr, csr, d_exp):
    bsz, seq, _ = x3.shape
    assert seq % (2 * SCAN_DEPTH * SSD_CHUNK) == 0
    grp = lambda n: pl.BlockSpec((1, seq, n), lambda b, g: (b, 0, g))
    heads_rows = pl.BlockSpec((2 * SSD_HEADS, seq), lambda b, g: (0, b))
    return pl.pallas_call(
        functools.partial(_ssd_scan_kernel, seq=seq),
        name="ssd_scan",
        grid=(bsz, SSD_GROUPS),
        in_specs=[grp(GS), grp(SSD_D_STATE), grp(SSD_D_STATE),
                  pl.BlockSpec((1, seq, LANES), lambda b, g: (b, 0, 0)),
                  heads_rows, heads_rows,
                  pl.BlockSpec((1, GS), lambda b, g: (0, g))],
        out_specs=pl.BlockSpec((1, seq, GS), lambda b, g: (b, 0, g)),
        out_shape=jax.ShapeDtypeStruct((bsz, seq, SSD_D_INNER), BF16),
        scratch_shapes=[pltpu.VMEM((seq, GS), F32),
                        pltpu.VMEM((SSD_D_STATE, GS), F32),
                        pltpu.VMEM((SSD_D_STATE, GS), F32),
                        pltpu.VMEM((2, 2 * LANES, SSD_HPG * SSD_CHUNK), BF16)],
        compiler_params=pltpu.CompilerParams(dimension_semantics=("arbitrary", "arbitrary"),
                                             vmem_limit_bytes=VMEM_LIMIT),
    )(x3, b3, c3, csc3, dtr, csr, d_exp)


def _attn_in_body(h, nw_ref, wqt_ref, wk_ref, wvt_ref, wz_ref, qn_ref, kn_ref, ones_ref,
                  qt_ref, k_ref, vt_ref, z_ref):
    tm = h.shape[0]
    xn = _rms_rows(h, nw_ref[...]).astype(BF16)
    inv_d = 1.0 / ATTN_HEAD_DIM

    yq = _dot_nt(wqt_ref[...], xn).reshape(-1, ATTN_HEAD_DIM, tm)
    ssq = jnp.sum(yq * yq, axis=1, keepdims=True)
    qn = (yq * lax.rsqrt(ssq * inv_d + RMS_EPS)).reshape(ATTN_QK_DIM, tm)
    qt_ref[...] = (qn * qn_ref[...]).astype(BF16)

    yk = _dot(xn, wk_ref[...])
    ssk = _dot((yk * yk).astype(BF16), ones_ref[...])
    k_ref[...] = (yk * lax.rsqrt(ssk * inv_d + RMS_EPS) * kn_ref[...]).astype(BF16)

    vt_ref[...] = _dot_nt(wvt_ref[...], xn).astype(BF16)
    z_ref[...] = _dot(xn, wz_ref[...]).astype(BF16)


def _ssd_out_attn_in_kernel(y_ref, zs_ref, snw_ref, wo_ref, h_ref, *rest):
    u = y_ref[...].astype(F32) * _silu(zs_ref[...].astype(F32))
    un = _rms_rows(u, snw_ref[...]).astype(BF16)
    h_new = h_ref[...] + _dot(un, wo_ref[...])
    attn_in_refs, (hn_ref, qt_ref, k_ref, vt_ref, z_ref) = rest[:-5], rest[-5:]
    hn_ref[...] = h_new
    _attn_in_body(h_new, *attn_in_refs, qt_ref, k_ref, vt_ref, z_ref)


def _ssd_out_attn_in(y2, zs2, snw, wo, h2, nw, wqt, wk, wvt, wz, qn_cols, kn, ones_bd):
    t = h2.shape[0]
    tm = qn_cols.shape[1]
    row = lambda n: pl.BlockSpec((tm, n), lambda i: (i, 0))
    col = lambda n: pl.BlockSpec((n, tm), lambda i: (0, i))
    attn_consts = (nw, wqt, wk, wvt, wz, qn_cols, kn, ones_bd)
    return pl.pallas_call(
        _ssd_out_attn_in_kernel,
        name="ssd_out_attn_in",
        grid=(t // tm,),
        in_specs=[row(SSD_D_INNER), row(SSD_D_INNER), _resident(snw.shape), _resident(wo.shape),
                  row(D_MODEL)] + [_resident(a.shape) for a in attn_consts],
        out_specs=[row(D_MODEL), col(ATTN_QK_DIM), row(ATTN_QK_DIM), col(ATTN_V_DIM),
                   row(ATTN_V_DIM)],
        out_shape=[jax.ShapeDtypeStruct((t, D_MODEL), F32),
                   jax.ShapeDtypeStruct((ATTN_QK_DIM, t), BF16),
                   jax.ShapeDtypeStruct((t, ATTN_QK_DIM), BF16),
                   jax.ShapeDtypeStruct((ATTN_V_DIM, t), BF16),
                   jax.ShapeDtypeStruct((t, ATTN_V_DIM), BF16)],
        compiler_params=pltpu.CompilerParams(dimension_semantics=("arbitrary",),
                                             vmem_limit_bytes=VMEM_LIMIT),
    )(y2, zs2, snw, wo, h2, *attn_consts)


def _attn_kernel(thr_ref, rb_ref, qt_ref, k_ref, vt_ref, z_ref, lam_ref, subln_ref, o_ref,
                 bias_ref, sa_ref, sb_ref, p_ref, m_ref, acc_ref, vte_ref, *, seq, tile, ktile,
                 lam_init):
    h = pl.program_id(0)
    b = pl.program_id(1)
    qi = pl.program_id(2)
    nk = seq // ktile
    kfac = ktile // tile
    half_buckets = REL_BUCKETS // 2
    sub = 8

    @pl.when((b == 0) & (qi == 0))
    def _build_bias():
        rel0 = (lax.broadcasted_iota(jnp.int32, (ktile, tile), 0)
                - lax.broadcasted_iota(jnp.int32, (ktile, tile), 1))
        for idx in range(kfac + 2):
            rel = rel0 + (idx - kfac) * tile
            n = jnp.abs(rel)
            neg = jnp.full((ktile, tile), rb_ref[h], F32)
            pos = jnp.full((ktile, tile), rb_ref[half_buckets * ATTN_HEADS + h], F32)
            for t in range(1, half_buckets):
                ge = n >= thr_ref[t]
                neg = jnp.where(ge, rb_ref[t * ATTN_HEADS + h], neg)
                pos = jnp.where(ge, rb_ref[(half_buckets + t) * ATTN_HEADS + h], pos)
            bias_ref[idx] = jnp.where(rel > 0, pos, neg) * LOG2E

    @pl.when(qi == 0)
    def _extend_v():
        vte_ref[0:ATTN_V_HEAD, :] = vt_ref[...]
        vte_ref[ATTN_V_HEAD:, :] = jnp.ones((ATTN_ONES_ROWS, seq), BF16)

    c_left = rb_ref[(half_buckets - 1) * ATTN_HEADS + h] * LOG2E
    c_right = rb_ref[(2 * half_buckets - 1) * ATTN_HEADS + h] * LOG2E

    def offset(kc):
        return kfac * kc - qi

    def is_near(kc):
        return (offset(kc) >= -kfac) & (offset(kc) <= 1)

    def far_bias(kc):
        return jnp.where(is_near(kc), 0.0, jnp.where(offset(kc) < 0, c_left, c_right))

    q_t = qt_ref[...]
    row = lax.broadcasted_iota(jnp.int32, (ATTN_V_HEAD, tile), 0)
    zero = jnp.zeros_like(q_t)
    qs = (jnp.where(row < ATTN_HEAD_DIM, q_t, zero), jnp.where(row >= ATTN_HEAD_DIM, q_t, zero))

    m_ref[...] = jnp.full(m_ref.shape, -jnp.inf, F32)
    acc_ref[...] = jnp.zeros(acc_ref.shape, F32)

    def scores(kc, t, s_out, with_bias):
        kt = k_ref[0, pl.ds(pl.multiple_of(kc * ktile, ktile), ktile), :]
        sb = _dot(kt, qs[t])
        if with_bias:
            sb = sb + bias_ref[jnp.clip(offset(kc) + kfac, 0, kfac + 1)]
        s_out[t] = sb
        m8 = sb[0:sub, :]
        for i in range(1, ktile // sub):
            m8 = jnp.maximum(m8, sb[i * sub:(i + 1) * sub, :])
        return m8

    def step_body(kc_next, s_next, kc, s_cur, m8s, with_bias):
        vt = vte_ref[:, pl.ds(pl.multiple_of(kc * ktile, ktile), ktile)]
        shift = far_bias(kc)
        m8s_next = []
        for t in range(2):
            m_old = m_ref[t]
            m_new = jnp.maximum(m_old, jnp.max(m8s[t], axis=0, keepdims=True) + shift)
            alpha = jnp.exp2(m_old - m_new)
            msub = m_new - shift
            if kc_next is not None:
                m8s_next.append(scores(kc_next, t, s_next, with_bias))
            for rb in range(ktile // ATTN_ROWS_B):
                rows = pl.ds(rb * ATTN_ROWS_B, ATTN_ROWS_B)
                p_ref[t, rows, :] = jnp.exp2(s_cur[t, rows, :] - msub).astype(BF16)
            acc_ref[t] = alpha * acc_ref[t] + _dot(vt, p_ref[t])
            m_ref[t] = m_new
        return tuple(m8s_next)

    def step(kc_next, s_next, kc, s_cur, m8s):
        return lax.cond(is_near(kc_next),
                        lambda: step_body(kc_next, s_next, kc, s_cur, m8s, True),
                        lambda: step_body(kc_next, s_next, kc, s_cur, m8s, False))

    def chunk_pair(j, m8s):
        c0 = 2 * j
        m8s_odd = step(c0 + 1, sb_ref, c0, sa_ref, m8s)
        return step(c0 + 2, sa_ref, c0 + 1, sb_ref, m8s_odd)

    def first_scores(with_bias):
        def fn():
            return tuple(scores(0, t, sa_ref, with_bias) for t in range(2))
        return fn

    m8s = lax.fori_loop(0, nk // 2 - 1, chunk_pair,
                        lax.cond(is_near(0), first_scores(True), first_scores(False)))
    m8s = step(nk - 1, sb_ref, nk - 2, sa_ref, m8s)
    step_body(None, None, nk - 1, sb_ref, m8s, False)

    lv = lam_ref[...]
    lam = (jnp.exp(jnp.sum(lv[0:1] * lv[1:2], axis=-1, keepdims=True))
           - jnp.exp(jnp.sum(lv[2:3] * lv[3:4], axis=-1, keepdims=True)) + lam_init)
    o1 = acc_ref[0, 0:ATTN_V_HEAD, :] / acc_ref[0, ATTN_V_HEAD:ATTN_V_HEAD + 1, :]
    o2 = acc_ref[1, 0:ATTN_V_HEAD, :] / acc_ref[1, ATTN_V_HEAD:ATTN_V_HEAD + 1, :]
    o = (o1 - lam * o2).T
    o = _rms_rows(o, subln_ref[...]) * (1.0 - lam_init)
    o_ref[0] = (o * _silu(z_ref[0].astype(F32))).astype(o_ref.dtype)


def _attention(thr, rb_flat, qt, k3, vt, z3, lam_vec, subln, lam_init):
    bsz, seq, _ = k3.shape
    tile = min(ATTN_TILE, seq)
    ktile = min(ATTN_KTILE, seq // 2)
    nq = seq // tile
    assert tile > REL_MAX_DIST and ktile % tile == 0 and seq % (2 * ktile) == 0
    assert ktile % ATTN_ROWS_B == 0
    smem = pl.BlockSpec(memory_space=pltpu.SMEM)
    qspec = pl.BlockSpec((ATTN_V_HEAD, tile), lambda h, b, i: (h, b * nq + i))
    kspec = pl.BlockSpec((1, seq, ATTN_V_HEAD), lambda h, b, i: (b, 0, h))
    vspec = pl.BlockSpec((ATTN_V_HEAD, seq), lambda h, b, i: (h, b))
    zspec = pl.BlockSpec((1, tile, ATTN_V_HEAD), lambda h, b, i: (b, i, h))
    return pl.pallas_call(
        functools.partial(_attn_kernel, seq=seq, tile=tile, ktile=ktile, lam_init=lam_init),
        name="diff_attention",
        grid=(ATTN_HEADS, bsz, nq),
        in_specs=[smem, smem, qspec, kspec, vspec, zspec,
                  pl.BlockSpec(lam_vec.shape, lambda h, b, i: (0, 0)),
                  pl.BlockSpec(subln.shape, lambda h, b, i: (0, 0))],
        out_specs=zspec,
        out_shape=jax.ShapeDtypeStruct((bsz, seq, ATTN_V_DIM), BF16),
        scratch_shapes=[pltpu.VMEM((ktile // tile + 2, ktile, tile), F32),
                        pltpu.VMEM((2, ktile, tile), F32),
                        pltpu.VMEM((2, ktile, tile), F32),
                        pltpu.VMEM((2, ktile, tile), BF16),
                        pltpu.VMEM((2, 1, tile), F32),
                        pltpu.VMEM((2, ATTN_V_HEAD + ATTN_ONES_ROWS, tile), F32),
                        pltpu.VMEM((ATTN_V_HEAD + ATTN_ONES_ROWS, seq), BF16)],
        compiler_params=pltpu.CompilerParams(
            dimension_semantics=("arbitrary", "arbitrary", "arbitrary"),
            vmem_limit_bytes=VMEM_LIMIT),
    )(thr, rb_flat, qt, k3, vt, z3, lam_vec, subln)


def _attn_out_kernel(o_ref, w_ref, h_ref, out_ref):
    out_ref[...] = h_ref[...] + _dot(o_ref[...], w_ref[...])


def _attn_out_proj(o2, w, h2):
    t = h2.shape[0]
    tm = min(ROW_TILE, t)
    row = lambda n: pl.BlockSpec((tm, n), lambda i: (i, 0))
    deep = lambda n: pl.BlockSpec((tm, n), lambda i: (i, 0), pipeline_mode=pl.Buffered(3))

    def outer(o_hbm, w_ref, h_hbm, out_hbm):
        pltpu.emit_pipeline(
            lambda o_ref, h_ref, out_ref: _attn_out_kernel(o_ref, w_ref, h_ref, out_ref),
            grid=(t // tm,),
            in_specs=[deep(ATTN_V_DIM), deep(D_MODEL)],
            out_specs=[row(D_MODEL)],
        )(o_hbm, h_hbm, out_hbm)

    return pl.pallas_call(
        outer,
        name="attn_out_proj",
        in_specs=[pl.BlockSpec(memory_space=pl.ANY),
                  pl.BlockSpec(memory_space=pltpu.VMEM),
                  pl.BlockSpec(memory_space=pl.ANY)],
        out_specs=pl.BlockSpec(memory_space=pl.ANY),
        out_shape=jax.ShapeDtypeStruct((t, D_MODEL), F32),
        compiler_params=pltpu.CompilerParams(vmem_limit_bytes=VMEM_LIMIT),
    )(o2, w, h2)


def _ssd_mixer(h2, bsz, seq, nw, w_in, conv_w, conv_b, dt_bias, a_log, d_skip):
    wb16 = w_in.astype(BF16)
    o_x = SSD_D_INNER
    o_b = 2 * SSD_D_INNER
    o_c = o_b + SSD_GN
    o_dt = o_c + SSD_GN
    wdt = wb16[:, o_dt:]
    wdc = jnp.concatenate([wdt, wdt], axis=1)
    perm = lambda a: a.reshape(2, SSD_GROUPS, SSD_HPG).transpose(1, 0, 2).reshape(2 * SSD_HEADS)
    wdr = wdt.T.reshape(2, SSD_GROUPS, SSD_HPG, D_MODEL).transpose(1, 0, 2, 3).reshape(
        2 * SSD_HEADS, D_MODEL)
    del o_b, o_c
    tm = min(ROW_TILE, seq)
    flat = lambda a: a.reshape(2 * SSD_HEADS).astype(F32)
    biasc = jnp.tile(flat(dt_bias), 2).reshape(1, LANES)
    alogc = jnp.tile(flat(a_log), 2).reshape(1, LANES)
    biasr = jnp.broadcast_to(perm(flat(dt_bias))[:, None], (2 * SSD_HEADS, tm))
    alogr = jnp.broadcast_to(perm(flat(a_log))[:, None], (2 * SSD_HEADS, tm))
    d_exp = jnp.repeat(d_skip.astype(F32), SSD_HEADDIM).reshape(1, SSD_D_INNER)
    z2, x2, b2, c2, csc, dtr, csr = _ssd_in_proj(
        h2, seq, nw.reshape(1, -1), wb16[:, :o_x], wb16[:, o_x:o_dt], wdc, wdr,
        conv_w.astype(F32), conv_b.reshape(1, -1).astype(F32), biasc, alogc, biasr, alogr)

    y3 = _ssd_scan(x2.reshape(bsz, seq, -1), b2.reshape(bsz, seq, -1), c2.reshape(bsz, seq, -1),
                   csc.reshape(bsz, seq, LANES), dtr, csr, d_exp)
    return y3.reshape(bsz * seq, -1), z2


def _t5_thresholds():
    nb = REL_BUCKETS // 2
    max_exact = nb // 2
    n = jnp.arange(REL_MAX_DIST + 1, dtype=jnp.int32)
    nf = jnp.maximum(n, 1).astype(F32)
    large = max_exact + (jnp.log(nf / max_exact) / math.log(REL_MAX_DIST / max_exact)
                         * (nb - max_exact)).astype(jnp.int32)
    bucket = jnp.where(n < max_exact, n, jnp.minimum(large, nb - 1))
    t = jnp.arange(nb, dtype=jnp.int32)
    return jnp.sum((bucket[None, :] < t[:, None]).astype(jnp.int32), axis=1)


def _attn_layer(y2, zs2, ssd_norm_w, ssd_w_out, h2, bsz, seq, nw, w_in, q_norm, k_norm, lam_vec,
                subln, w_out, thr, rb_flat, ones_bd, layer_idx):
    wb16 = w_in.astype(BF16)
    reps = ATTN_QK_DIM // ATTN_HEAD_DIM
    tm = min(ROW_TILE, bsz * seq)
    qn = jnp.tile(q_norm.astype(F32), reps) * (ATTN_HEAD_DIM ** -0.5 * LOG2E)
    qn_cols = jnp.broadcast_to(qn[:, None], (ATTN_QK_DIM, tm))
    kn = jnp.tile(k_norm.astype(F32), reps).reshape(1, -1)
    h2, qt, k2, vt, z2 = _ssd_out_attn_in(
        y2, zs2, ssd_norm_w.reshape(1, -1).astype(F32), ssd_w_out.astype(BF16), h2,
        nw.reshape(1, -1), wb16[:, :ATTN_QK_DIM].T, wb16[:, ATTN_QK_DIM:2 * ATTN_QK_DIM],
        wb16[:, 2 * ATTN_QK_DIM:2 * ATTN_QK_DIM + ATTN_V_DIM].T,
        wb16[:, 2 * ATTN_QK_DIM + ATTN_V_DIM:], qn_cols, kn, ones_bd)
    lam_init = 0.8 - 0.6 * math.exp(-0.3 * layer_idx)
    r3 = lambda a: a.reshape(bsz, seq, -1)
    o3 = _attention(thr, rb_flat, qt, r3(k2), vt, r3(z2), lam_vec.astype(F32),
                    subln.reshape(1, -1).astype(F32), lam_init)
    return _attn_out_proj(o3.reshape(bsz * seq, -1), w_out.astype(BF16), h2)


def kernel(x, norm_w, ssd_w_in, ssd_conv_w, ssd_conv_b, ssd_dt_bias, ssd_a_log, ssd_d, ssd_norm_w, ssd_w_out, attn_w_in, attn_q_norm, attn_k_norm, attn_lambda, attn_subln, attn_w_out, rel_bias):
    bsz, seq, _ = x.shape
    depth = norm_w.shape[0]
    h2 = x.reshape(bsz * seq, D_MODEL)
    thr = _t5_thresholds()
    rb_flat = rel_bias.astype(F32).reshape(-1)
    blk = jnp.arange(ATTN_QK_DIM, dtype=jnp.int32) // ATTN_HEAD_DIM
    ones_bd = (blk[:, None] == blk[None, :]).astype(BF16)
    assert depth % 2 == 0
    for j in range(depth // 2):
        y2, zs2 = _ssd_mixer(h2, bsz, seq, norm_w[2 * j], ssd_w_in[j], ssd_conv_w[j], ssd_conv_b[j],
                             ssd_dt_bias[j], ssd_a_log[j], ssd_d[j])
        h2 = _attn_layer(y2, zs2, ssd_norm_w[j], ssd_w_out[j], h2, bsz, seq, norm_w[2 * j + 1],
                         attn_w_in[j], attn_q_norm[j], attn_k_norm[j], attn_lambda[j],
                         attn_subln[j], attn_w_out[j], thr, rb_flat, ones_bd, 2 * j + 1)
    return h2.reshape(bsz, seq, D_MODEL)
```
